```python
import jax, jax.numpy as jnp
from jax import lax
import numpy as np

D_MODEL = 2048
BATCH = 4
SEQ = 4096
DEPTH = 2

GRID_W = 64
CTX_LEN = 256
N_MIXERS = 2
N_HEADS = 16
N_KV_HEADS = 4
HEAD_DIM = D_MODEL // N_HEADS
KV_GROUP = N_HEADS // N_KV_HEADS
KV_DIM = N_KV_HEADS * HEAD_DIM
QKV_DIM = D_MODEL + 2 * KV_DIM
ROPE_THETA = 10000.0
Q_BLOCK = 128
N_FOURIER_GROUPS = 8
FOURIER_GROUP_DIM = D_MODEL // N_FOURIER_GROUPS
D_FF = 4 * D_MODEL
N_MOD = 6
EPS = 1e-6
N_ATTN_LAYERS = (DEPTH + N_MIXERS - 1) // N_MIXERS
N_FOURIER_LAYERS = DEPTH // N_MIXERS

kernel_name = "hybrid_gqa_fourier_dit_block"


def rms_norm(x, g):
    xf = x.astype(jnp.float32)
    y = xf * lax.rsqrt(jnp.mean(xf * xf, axis=-1, keepdims=True) + EPS)
    return (y * g.astype(jnp.float32)).astype(x.dtype)


def modulate(h, shift, scale):
    return h * (1 + scale) + shift


def axial_rope_angles(n_tokens):
    rows = n_tokens // GRID_W
    row = jnp.broadcast_to(jnp.arange(rows)[:, None], (rows, GRID_W)).reshape(-1)
    col = jnp.broadcast_to(jnp.arange(GRID_W)[None, :], (rows, GRID_W)).reshape(-1)
    n_freq = HEAD_DIM // 4
    inv_freq = ROPE_THETA ** (-jnp.arange(n_freq, dtype=jnp.float32) / n_freq)
    ang = jnp.concatenate([row.astype(jnp.float32)[:, None] * inv_freq,
                           col.astype(jnp.float32)[:, None] * inv_freq], axis=-1)
    return jnp.cos(ang), jnp.sin(ang)


def apply_rope(x, cos, sin):
    xp = x.astype(jnp.float32).reshape(x.shape[:-1] + (HEAD_DIM // 2, 2))
    x0, x1 = xp[..., 0], xp[..., 1]
    c = cos[None, :, None, :]
    s = sin[None, :, None, :]
    out = jnp.stack([x0 * c - x1 * s, x0 * s + x1 * c], axis=-1)
    return out.reshape(x.shape).astype(x.dtype)


def project_kv(h, w_kv, k_gain):
    b, n, _ = h.shape
    kv = h @ w_kv
    k, v = jnp.split(kv, 2, axis=-1)
    k = rms_norm(k.reshape(b, n, N_KV_HEADS, HEAD_DIM), k_gain)
    v = v.reshape(b, n, N_KV_HEADS, HEAD_DIM)
    return k, v


def project_qkv(h, w_qkv, q_gain, k_gain):
    b, n, _ = h.shape
    q = rms_norm((h @ w_qkv[:, :D_MODEL]).reshape(b, n, N_HEADS, HEAD_DIM), q_gain)
    k, v = project_kv(h, w_qkv[:, D_MODEL:], k_gain)
    return q, k, v


def latent_attention(q_lat, k_lat, v_lat, k_ctx, v_ctx):
    b, s = q_lat.shape[:2]
    n_blocks = s // Q_BLOCK
    qb = jnp.moveaxis(q_lat.reshape(b, n_blocks, Q_BLOCK, N_KV_HEADS, KV_GROUP, HEAD_DIM), 1, 0)
    scale = HEAD_DIM ** -0.5

    def one_block(q):
        logits = jnp.concatenate([
            jnp.einsum('bqkgd,bskd->bkgqs', q, k_lat),
            jnp.einsum('bqkgd,bskd->bkgqs', q, k_ctx)], axis=-1).astype(jnp.float32) * scale
        p = jax.nn.softmax(logits, axis=-1).astype(v_lat.dtype)
        return (jnp.einsum('bkgqs,bskd->bqkgd', p[..., :s], v_lat)
                + jnp.einsum('bkgqs,bskd->bqkgd', p[..., s:], v_ctx))

    o = lax.map(one_block, qb)
    return jnp.moveaxis(o, 0, 1).reshape(b, s, D_MODEL)


def context_attention(q_ctx, k_ctx, v_ctx):
    b, l = q_ctx.shape[:2]
    q = q_ctx.reshape(b, l, N_KV_HEADS, KV_GROUP, HEAD_DIM)
    logits = jnp.einsum('bqkgd,bskd->bkgqs', q, k_ctx).astype(jnp.float32) * (HEAD_DIM ** -0.5)
    p = jax.nn.softmax(logits, axis=-1).astype(v_ctx.dtype)
    return jnp.einsum('bkgqs,bskd->bqkgd', p, v_ctx).reshape(b, l, D_MODEL)


def fourier_mix(h, w_o):
    b, n, _ = h.shape
    hg = h.astype(jnp.float32).reshape(b, n, N_FOURIER_GROUPS, FOURIER_GROUP_DIM)
    y = jnp.fft.fftn(hg, axes=(1, 3), norm="ortho").real
    return y.reshape(b, n, D_MODEL).astype(h.dtype) @ w_o


def sq_relu_mlp(h, w1, w2):
    return jnp.square(jax.nn.relu(h @ w1)) @ w2


def setup_inputs(seed: int = 0) -> dict:
    key = jax.random.key(seed)
    ks = jax.random.split(key, 16)
    f32 = jnp.float32

    def normal(k, shape, std):
        return jax.random.normal(k, shape, f32) * std

    return {
        "x": normal(ks[0], (BATCH, SEQ, D_MODEL), 1.0),
        "c": normal(ks[1], (BATCH, D_MODEL), 1.0),
        "ctx": normal(ks[2], (BATCH, CTX_LEN, D_MODEL), 1.0),
        "c_ctx": normal(ks[3], (D_MODEL,), 1.0),
        "ada_w": normal(ks[4], (DEPTH, D_MODEL, N_MOD * D_MODEL), 0.02),
        "ada_b": normal(ks[5], (DEPTH, N_MOD * D_MODEL), 0.05),
        "norm_mix_g": 1.0 + normal(ks[6], (DEPTH, D_MODEL), 0.02),
        "norm_mlp_g": 1.0 + normal(ks[7], (DEPTH, D_MODEL), 0.02),
        "attn_w_qkv": normal(ks[8], (N_ATTN_LAYERS, D_MODEL, QKV_DIM), D_MODEL ** -0.5),
        "attn_q_norm_g": 1.0 + normal(ks[9], (N_ATTN_LAYERS, HEAD_DIM), 0.02),
        "attn_k_norm_g": 1.0 + normal(ks[10], (N_ATTN_LAYERS, HEAD_DIM), 0.02),
        "attn_w_o": normal(ks[11], (N_ATTN_LAYERS, D_MODEL, D_MODEL), D_MODEL ** -0.5),
        "fourier_w_o": normal(ks[12], (N_FOURIER_LAYERS, D_MODEL, D_MODEL), D_MODEL ** -0.5),
        "mlp_w1": normal(ks[13], (DEPTH, D_MODEL, D_FF), D_MODEL ** -0.5),
        "mlp_w2": normal(ks[14], (DEPTH, D_FF, D_MODEL), D_FF ** -0.5),
        "final_norm_g": 1.0 + normal(ks[15], (D_MODEL,), 0.02),
    }


def reference(x, c, ctx, c_ctx, ada_w, ada_b, norm_mix_g, norm_mlp_g, attn_w_qkv, attn_q_norm_g,
              attn_k_norm_g, attn_w_o, fourier_w_o, mlp_w1, mlp_w2, final_norm_g):
    cos, sin = axial_rope_angles(x.shape[1])
    silu_c = jax.nn.silu(c)
    silu_cc = jax.nn.silu(c_ctx)
    for i in range(DEPTH):
        mixer = i % N_MIXERS
        j = i // N_MIXERS
        ctx_read_later = any(l % N_MIXERS == 0 for l in range(i + 1, DEPTH))
        mod_lat = (silu_c @ ada_w[i] + ada_b[i])[:, None, :]
        sh1, sc1, g1, sh2, sc2, g2 = jnp.split(mod_lat, N_MOD, axis=-1)
        mod_ctx = silu_cc @ ada_w[i] + ada_b[i]
        csh1, csc1, cg1, csh2, csc2, cg2 = jnp.split(mod_ctx, N_MOD, axis=-1)

        h_lat = modulate(rms_norm(x, norm_mix_g[i]), sh1, sc1)
        if mixer == 0 or ctx_read_later:
            h_ctx = modulate(rms_norm(ctx, norm_mix_g[i]), csh1, csc1)

        if mixer == 0:
            q_l, k_l, v_l = project_qkv(h_lat, attn_w_qkv[j], attn_q_norm_g[j], attn_k_norm_g[j])
            q_l = apply_rope(q_l, cos, sin)
            k_l = apply_rope(k_l, cos, sin)
            if ctx_read_later:
                q_c, k_c, v_c = project_qkv(h_ctx, attn_w_qkv[j], attn_q_norm_g[j], attn_k_norm_g[j])
                o_ctx = context_attention(q_c, k_c, v_c) @ attn_w_o[j]
            else:
                k_c, v_c = project_kv(h_ctx, attn_w_qkv[j][:, D_MODEL:], attn_k_norm_g[j])
            o_lat = latent_attention(q_l, k_l, v_l, k_c, v_c) @ attn_w_o[j]
        else:
            o_lat = fourier_mix(h_lat, fourier_w_o[j])
            if ctx_read_later:
                o_ctx = fourier_mix(h_ctx, fourier_w_o[j])

        x = x + g1 * o_lat
        x = x + g2 * sq_relu_mlp(modulate(rms_norm(x, norm_mlp_g[i]), sh2, sc2), mlp_w1[i], mlp_w2[i])
        if ctx_read_later:
            ctx = ctx + cg1 * o_ctx
            ctx = ctx + cg2 * sq_relu_mlp(modulate(rms_norm(ctx, norm_mlp_g[i]), csh2, csc2),
                                          mlp_w1[i], mlp_w2[i])
    return rms_norm(x, final_norm_g)
```

```python
import functools

import numpy as np
import jax
import jax.numpy as jnp
from jax import lax
from jax.experimental import pallas as pl
from jax.experimental.pallas import tpu as pltpu

HEAD_DIM = 128
KV_GROUP = 4
GRID_W = 64
ROPE_THETA = 10000.0
FOURIER_GROUP_DIM = 256
N_MOD = 6
EPS = 1e-6
RADIX = 4
LOG2_E = 1.4426950408889634
VMEM_LIMIT_BYTES = 56 * 1024 * 1024

F32 = jnp.float32
BF16 = jnp.bfloat16


def _params(*sem):
    return pltpu.CompilerParams(dimension_semantics=sem, vmem_limit_bytes=VMEM_LIMIT_BYTES)


def _resident(shape, index_map):
    return pl.BlockSpec(shape, index_map, pipeline_mode=pl.Buffered(1))


def _rms_scale(x):
    return lax.rsqrt(jnp.mean(x * x, axis=-1, keepdims=True) + EPS)


def _mod_kernel(c_ref, w_ref, b_ref, o_ref):
    c = c_ref[...]
    s = c / (1.0 + jnp.exp(-c))
    o_ref[...] = jnp.dot(s, w_ref[...], preferred_element_type=F32) + b_ref[...]


def _modulation(c_rows, ada_w, ada_b, tn):
    depth, d, n = ada_w.shape
    return pl.pallas_call(
        _mod_kernel,
        grid=(depth, n // tn),
        in_specs=[
            pl.BlockSpec((8, d), lambda l, j: (0, 0)),
            pl.BlockSpec((None, d, tn), lambda l, j: (l, 0, j)),
            pl.BlockSpec((None, 1, tn), lambda l, j: (l, 0, j)),
        ],
        out_specs=pl.BlockSpec((None, 8, tn), lambda l, j: (l, 0, j)),
        out_shape=jax.ShapeDtypeStruct((depth, 8, n), F32),
        compiler_params=_params("parallel", "parallel"),
        name="adaln_modulation",
    )(c_rows, ada_w, ada_b.reshape(depth, 1, n))


def _mod_spec(d, layer, which, row=None):
    return pl.BlockSpec((None, None, None, 1, d),
                        lambda *g: (layer, g[0] if row is None else row, which, 0, 0))


def _qkv_kernel(x_ref, sh_ref, sc_ref, gn_ref, w_ref, qg_ref, kg_ref, cos_ref, sin_ref, *out_refs,
                n_q_heads, n_kv_heads, rope, q_scale):
    x = x_ref[...]
    h = (x * _rms_scale(x) * (gn_ref[...] * (1.0 + sc_ref[...])) + sh_ref[...]).astype(BF16)
    if rope:
        cosf, sinf = cos_ref[...], sin_ref[...]
        even = (lax.broadcasted_iota(jnp.int32, cosf.shape, 1) % 2) == 0

    def head_norm_rope(t, gain):
        t = t * _rms_scale(t) * gain
        if rope:
            partner = jnp.where(even, pltpu.roll(t, HEAD_DIM - 1, 1), pltpu.roll(t, 1, 1))
            t = t * cosf + partner * sinf
        return t

    if n_q_heads:
        q_ref, k_ref, v_ref = out_refs
    else:
        k_ref, v_ref = out_refs
    group_w = KV_GROUP * HEAD_DIM
    col = 0
    for g in range(n_q_heads // KV_GROUP):
        acc = jnp.dot(h, w_ref[:, col:col + group_w], preferred_element_type=F32)
        for i in range(KV_GROUP):
            t = head_norm_rope(acc[:, i * HEAD_DIM:(i + 1) * HEAD_DIM], qg_ref[...])
            q_ref[g * KV_GROUP + i] = (t * q_scale).astype(BF16)
        col += group_w
    kw = n_kv_heads * HEAD_DIM
    acc = jnp.dot(h, w_ref[:, col:col + kw], preferred_element_type=F32)
    for i in range(n_kv_heads):
        k_ref[i] = head_norm_rope(acc[:, i * HEAD_DIM:(i + 1) * HEAD_DIM], kg_ref[...]).astype(BF16)
    col += kw
    acc = jnp.dot(h, w_ref[:, col:col + kw], preferred_element_type=F32)
    for i in range(n_kv_heads):
        v_ref[i] = acc[:, i * HEAD_DIM:(i + 1) * HEAD_DIM].astype(BF16)


def _qkv_project(x, mod, layer, mod_row, gn, w, qg, kg, cosf, sinf, *, n_q_heads, n_kv_heads, rope, ts):
    b, s, d = x.shape
    ncols = w.shape[1]
    head_out = lambda nh: pl.BlockSpec((None, nh, ts, HEAD_DIM), lambda bi, i: (bi, 0, i, 0))
    head_shape = lambda nh: jax.ShapeDtypeStruct((b, nh, s, HEAD_DIM), BF16)
    out_specs = [head_out(n_kv_heads), head_out(n_kv_heads)]
    out_shape = [head_shape(n_kv_heads), head_shape(n_kv_heads)]
    if n_q_heads:
        out_specs.insert(0, head_out(n_q_heads))
        out_shape.insert(0, head_shape(n_q_heads))
    kernel = functools.partial(_qkv_kernel, n_q_heads=n_q_heads, n_kv_heads=n_kv_heads, rope=rope,
                               q_scale=HEAD_DIM ** -0.5 * LOG2_E)
    return pl.pallas_call(
        kernel,
        grid=(b, s // ts),
        in_specs=[
            pl.BlockSpec((None, ts, d), lambda bi, i: (bi, i, 0)),
            _mod_spec(d, layer, 0, mod_row),
            _mod_spec(d, layer, 1, mod_row),
            pl.BlockSpec((1, d), lambda bi, i: (0, 0)),
            _resident((d, ncols), lambda bi, i: (0, 0)),
            pl.BlockSpec((1, HEAD_DIM), lambda bi, i: (0, 0)),
            pl.BlockSpec((1, HEAD_DIM), lambda bi, i: (0, 0)),
            pl.BlockSpec((ts, HEAD_DIM), lambda bi, i: (i, 0)),
            pl.BlockSpec((ts, HEAD_DIM), lambda bi, i: (i, 0)),
        ],
        out_specs=out_specs,
        out_shape=out_shape,
        compiler_params=_params("parallel", "parallel"),
        name="qkv_project" if n_q_heads else "kv_project_ctx",
    )(x, mod, mod, gn, w, qg, kg, cosf, sinf)


def _attn_kernel(q_ref, kl_ref, vl_ref, kc_ref, vc_ref, o_ref, *, tkv):
    g, tq, hd = q_ref.shape
    m_rows = g * tq
    q = q_ref[...].reshape(m_rows, hd)

    def step(k, v, carry):
        m, l, acc = carry
        s = lax.dot_general(q, k, (((1,), (1,)), ((), ())), preferred_element_type=F32)
        m_new = jnp.maximum(m, jnp.max(s, axis=-1, keepdims=True))
        alpha = jnp.exp2(m - m_new)
        p = jnp.exp2(s - m_new)
        l = alpha * l + jnp.sum(p, axis=-1, keepdims=True)
        acc = alpha * acc + jnp.dot(p.astype(BF16), v, preferred_element_type=F32)
        return m_new, l, acc

    def latent_step(i, carry):
        off = pl.multiple_of(i * tkv, tkv)
        return step(kl_ref[pl.ds(off, tkv), :], vl_ref[pl.ds(off, tkv), :], carry)

    carry = (jnp.full((m_rows, 1), -jnp.inf, F32), jnp.zeros((m_rows, 1), F32),
             jnp.zeros((m_rows, hd), F32))
    carry = lax.fori_loop(0, kl_ref.shape[0] // tkv, latent_step, carry)
    _, l, acc = step(kc_ref[...], vc_ref[...], carry)
    o = acc / l
    for i in range(g):
        o_ref[:, i * hd:(i + 1) * hd] = o[i * tq:(i + 1) * tq].astype(o_ref.dtype)


def _attention(q, k_lat, v_lat, k_ctx, v_ctx, *, tq, tkv):
    b, h, s, hd = q.shape
    kvh = h // KV_GROUP
    n_ctx = k_ctx.shape[2]
    kv_spec = lambda n: pl.BlockSpec((None, None, n, hd), lambda bi, j, i: (bi, j, 0, 0))
    return pl.pallas_call(
        functools.partial(_attn_kernel, tkv=tkv),
        grid=(b, kvh, s // tq),
        in_specs=[
            pl.BlockSpec((None, KV_GROUP, tq, hd), lambda bi, j, i: (bi, j, i, 0)),
            kv_spec(s), kv_spec(s), kv_spec(n_ctx), kv_spec(n_ctx),
        ],
        out_specs=pl.BlockSpec((None, tq, KV_GROUP * hd), lambda bi, j, i: (bi, i, j)),
        out_shape=jax.ShapeDtypeStruct((b, s, h * hd), BF16),
        compiler_params=_params("parallel", "parallel", "parallel"),
        name="gqa_attention",
    )(q, k_lat, v_lat, k_ctx, v_ctx)


def _oproj_kernel(a_ref, w_ref, x_ref, g_ref, gn_ref, sc_ref, sh_ref, x1_ref, h_ref):
    y = jnp.dot(a_ref[...], w_ref[...], preferred_element_type=F32)
    x1 = x_ref[...] + g_ref[...] * y
    x1_ref[...] = x1
    h_ref[...] = (x1 * _rms_scale(x1) * (gn_ref[...] * (1.0 + sc_ref[...])) + sh_ref[...]).astype(BF16)


def _out_project(a, w, x, mod, layer, gn, *, tm):
    b, s, d = x.shape
    tile = pl.BlockSpec((None, tm, d), lambda bi, i: (bi, i, 0))
    return pl.pallas_call(
        _oproj_kernel,
        grid=(b, s // tm),
        in_specs=[
            tile,
            _resident((d, d), lambda bi, i: (0, 0)),
            tile,
            _mod_spec(d, layer, 2),
            pl.BlockSpec((1, d), lambda bi, i: (0, 0)),
            _mod_spec(d, layer, 4),
            _mod_spec(d, layer, 3),
        ],
        out_specs=[tile, tile],
        out_shape=[jax.ShapeDtypeStruct((b, s, d), F32), jax.ShapeDtypeStruct((b, s, d), BF16)],
        compiler_params=_params("parallel", "parallel"),
        name="out_project",
    )(a, w, x, mod, gn, mod, mod)


def _mlp_kernel(h_ref, w1_ref, w2_ref, x1_ref, g2_ref, gn_ref, sc_ref, sh_ref, *refs, final):
    out_refs, acc_ref = refs[:-1], refs[-1]
    c = pl.program_id(2)

    @pl.when(c == 0)
    def _():
        acc_ref[...] = jnp.zeros_like(acc_ref)

    u = jnp.maximum(jnp.dot(h_ref[...], w1_ref[...], preferred_element_type=F32), 0.0)
    acc_ref[...] += jnp.dot((u * u).astype(BF16), w2_ref[...], preferred_element_type=F32)

    @pl.when(c == pl.num_programs(2) - 1)
    def _():
        x2 = x1_ref[...] + g2_ref[...] * acc_ref[...]
        normed = x2 * _rms_scale(x2)
        if final:
            out_refs[0][...] = normed * gn_ref[...]
        else:
            out_refs[0][...] = x2
            out_refs[1][...] = (normed * (gn_ref[...] * (1.0 + sc_ref[...])) + sh_ref[...]).astype(BF16)


def _mlp(h, w1, w2, x1, mod, layer, gn_next, next_layer, *, tm, tf):
    b, s, d = x1.shape
    f = w1.shape[1]
    final = next_layer is None
    tile = pl.BlockSpec((None, tm, d), lambda bi, i, c: (bi, i, 0))
    row = pl.BlockSpec((1, d), lambda bi, i, c: (0, 0))
    nl = layer if final else next_layer
    out_shape = [jax.ShapeDtypeStruct((b, s, d), F32)]
    if not final:
        out_shape.append(jax.ShapeDtypeStruct((b, s, d), BF16))
    return pl.pallas_call(
        functools.partial(_mlp_kernel, final=final),
        grid=(b, s // tm, f // tf),
        in_specs=[
            tile,
            pl.BlockSpec((d, tf), lambda bi, i, c: (0, c)),
            pl.BlockSpec((tf, d), lambda bi, i, c: (c, 0)),
            tile,
            _mod_spec(d, layer, 5),
            row,
            _mod_spec(d, nl, 1),
            _mod_spec(d, nl, 0),
        ],
        out_specs=[tile] * len(out_shape),
        out_shape=out_shape,
        scratch_shapes=[pltpu.VMEM((tm, d), F32)],
        compiler_params=_params("parallel", "parallel", "arbitrary"),
        name="mlp_final" if final else "mlp",
    )(h, w1, w2, x1, mod, gn_next, mod, mod)


def _seq_dft_mats(s, tk):
    n = s // RADIX
    k1 = np.arange(n, dtype=np.int64)[:, None]
    m = np.arange(n, dtype=np.int64)[None, :]
    mats = []
    for j in range(RADIX):
        ang = 2.0 * np.pi * (((RADIX * m + j) * k1) % s) / s
        c = (np.cos(ang) / np.sqrt(s)).reshape(n // tk, tk, n)
        sn = (-np.sin(ang) / np.sqrt(s)).reshape(n // tk, tk, n)
        mats.append(np.concatenate([c, sn], axis=1))
    return np.stack(mats).astype(np.float32)


def _chan_dft_mat():
    dg = FOURIER_GROUP_DIM
    idx = np.arange(dg, dtype=np.int64)
    ang = 2.0 * np.pi * ((idx[:, None] * idx[None, :]) % dg) / dg
    return (np.concatenate([np.cos(ang), np.sin(ang)], axis=0) / np.sqrt(dg)).astype(np.float32)


def _fourier_kernel(h0_ref, h1_ref, h2_ref, h3_ref, m_ref, cs_ref, o_ref):
    tk = m_ref.shape[1] // 2
    a, bb = [], []
    for j, h_ref in enumerate((h0_ref, h1_ref, h2_ref, h3_ref)):
        y = jnp.dot(m_ref[j], h_ref[...], preferred_element_type=F32)
        a.append(y[:tk])
        bb.append(y[tk:])
    e0, e1 = a[0] + a[2], a[0] - a[2]
    f0, f1 = bb[0] + bb[2], bb[0] - bb[2]
    o0, o1 = a[1] + a[3], a[1] - a[3]
    q0, q1 = bb[1] + bb[3], bb[1] - bb[3]
    pr = (e0 + o0, e1 + q1, e0 - o0, e1 - q1)
    pi = (f0 + q0, f1 - o1, f0 - q0, f1 + o1)
    dg = FOURIER_GROUP_DIM
    cs = cs_ref[...]
    for k2 in range(RADIX):
        prb, pib = pr[k2].astype(BF16), pi[k2].astype(BF16)
        for gi in range(o_ref.shape[-1] // dg):
            sl = slice(gi * dg, (gi + 1) * dg)
            y = (jnp.dot(prb[:, sl], cs[:dg], preferred_element_type=F32)
                 + jnp.dot(pib[:, sl], cs[dg:], preferred_element_type=F32))
            o_ref[k2, :, sl] = y.astype(o_ref.dtype)


def _fourier_mix(h, *, tk, tn):
    b, s, d = h.shape
    n = s // RADIX
    ncol = d // tn
    mats = jnp.asarray(_seq_dft_mats(s, tk)).astype(BF16)
    cs = jnp.asarray(_chan_dft_mat()).astype(BF16)
    h4 = h.reshape(b, n, RADIX * d)
    h_spec = lambda j: pl.BlockSpec((None, n, tn), lambda bi, c, t: (bi, 0, j * ncol + c))
    y = pl.pallas_call(
        _fourier_kernel,
        grid=(b, ncol, n // tk),
        in_specs=[
            h_spec(0), h_spec(1), h_spec(2), h_spec(3),
            pl.BlockSpec((RADIX, None, 2 * tk, n), lambda bi, c, t: (0, t, 0, 0)),
            pl.BlockSpec((2 * FOURIER_GROUP_DIM, FOURIER_GROUP_DIM), lambda bi, c, t: (0, 0)),
        ],
        out_specs=pl.BlockSpec((None, RADIX, tk, tn), lambda bi, c, t: (bi, 0, t, c)),
        out_shape=jax.ShapeDtypeStruct((b, RADIX, n, d), BF16),
        compiler_params=_params("parallel", "parallel", "parallel"),
        name="fourier_mix",
    )(h4, h4, h4, h4, mats, cs)
    return y.reshape(b, s, d)


def _rope_tables(s):
    rows = s // GRID_W
    row = jnp.broadcast_to(jnp.arange(rows)[:, None], (rows, GRID_W)).reshape(-1)
    col = jnp.broadcast_to(jnp.arange(GRID_W)[None, :], (rows, GRID_W)).reshape(-1)
    n_freq = HEAD_DIM // 4
    inv_freq = ROPE_THETA ** (-jnp.arange(n_freq, dtype=F32) / n_freq)
    ang = jnp.concatenate([row.astype(F32)[:, None] * inv_freq, col.astype(F32)[:, None] * inv_freq], axis=-1)
    cos, sin = jnp.cos(ang), jnp.sin(ang)
    cosf = jnp.repeat(cos, 2, axis=-1)
    sinf = jnp.stack([-sin, sin], axis=-1).reshape(s, HEAD_DIM)
    return cosf, sinf


def _tiles(s, d):
    return dict(
        ts=min(512, s), tq=min(256, s), tkv=min(512, s), tm_o=min(256, s), tm=min(512, s),
        tk=min(256, s // RADIX), tn=min(512, d),
    )


def kernel(x, c, ctx, c_ctx, ada_w, ada_b, norm_mix_g, norm_mlp_g, attn_w_qkv, attn_q_norm_g, attn_k_norm_g,
           attn_w_o, fourier_w_o, mlp_w1, mlp_w2, final_norm_g):
    b, s, d = x.shape
    depth = ada_w.shape[0]
    assert depth == 2 and d % (KV_GROUP * HEAD_DIM) == 0 and s % GRID_W == 0 and b <= 4
    n_heads = d // HEAD_DIM
    n_kv = n_heads // KV_GROUP
    f = mlp_w1.shape[2]
    t = _tiles(s, d)

    c_rows = jnp.zeros((8, d), F32).at[:b].set(c).at[4].set(c_ctx)
    mod = _modulation(c_rows, ada_w, ada_b, tn=min(1024, N_MOD * d)).reshape(depth, 8, N_MOD, 1, d)

    w_qkv = attn_w_qkv[0].astype(BF16)
    w_ao = attn_w_o[0].astype(BF16)
    w_fo = fourier_w_o[0].astype(BF16)
    w1 = mlp_w1.astype(BF16)
    w2 = mlp_w2.astype(BF16)
    gmix = norm_mix_g.reshape(depth, 1, d)
    gmlp = norm_mlp_g.reshape(depth, 1, d)
    qg = attn_q_norm_g[0].reshape(1, HEAD_DIM)
    kg = attn_k_norm_g[0].reshape(1, HEAD_DIM)
    cosf, sinf = _rope_tables(s)

    q, k_lat, v_lat = _qkv_project(x, mod, 0, None, gmix[0], w_qkv, qg, kg, cosf, sinf,
                                   n_q_heads=n_heads, n_kv_heads=n_kv, rope=True, ts=t["ts"])
    n_ctx = ctx.shape[1]
    k_ctx, v_ctx = _qkv_project(ctx, mod, 0, 4, gmix[0], w_qkv[:, d:], qg, kg, cosf[:n_ctx], sinf[:n_ctx],
                                n_q_heads=0, n_kv_heads=n_kv, rope=False, ts=n_ctx)
    o = _attention(q, k_lat, v_lat, k_ctx, v_ctx, tq=t["tq"], tkv=t["tkv"])
    x1, h = _out_project(o, w_ao, x, mod, 0, gmlp[0], tm=t["tm_o"])
    x2, h = _mlp(h, w1[0], w2[0], x1, mod, 0, gmix[1], 1, tm=t["tm"], tf=min(512, f))

    y = _fourier_mix(h, tk=t["tk"], tn=t["tn"])
    x1, h = _out_project(y, w_fo, x2, mod, 1, gmlp[1], tm=t["tm_o"])
    (out,) = _mlp(h, w1[1], w2[1], x1, mod, 1, final_norm_g.reshape(1, d), None, tm=t["tm"], tf=min(512, f))
    return out
```

```python
import functools

import numpy as np
import jax
import jax.numpy as jnp
from jax import lax
from jax.experimental import pallas as pl
from jax.experimental.pallas import tpu as pltpu

HEAD_DIM = 128
KV_GROUP = 4
GRID_W = 64
ROPE_THETA = 10000.0
FOURIER_GROUP_DIM = 256
N_MOD = 6
EPS = 1e-6
RADIX = 4
LOG2_E = 1.4426950408889634
VMEM_LIMIT_BYTES = 56 * 1024 * 1024

F32 = jnp.float32
BF16 = jnp.bfloat16


def _params(*sem):
    return pltpu.CompilerParams(dimension_semantics=sem, vmem_limit_bytes=VMEM_LIMIT_BYTES)


def _resident(shape, index_map):
    return pl.BlockSpec(shape, index_map, pipeline_mode=pl.Buffered(1))


def _rms_scale(x):
    return lax.rsqrt(jnp.mean(x * x, axis=-1, keepdims=True) + EPS)


def _mod_kernel(c_ref, w_ref, b_ref, o_ref):
    c = c_ref[...]
    s = c / (1.0 + jnp.exp(-c))
    o_ref[...] = jnp.dot(s, w_ref[...], preferred_element_type=F32) + b_ref[...]


def _modulation(c_rows, ada_w, ada_b, tn):
    depth, d, n = ada_w.shape
    return pl.pallas_call(
        _mod_kernel,
        grid=(depth, n // tn),
        in_specs=[
            pl.BlockSpec((8, d), lambda l, j: (0, 0)),
            pl.BlockSpec((None, d, tn), lambda l, j: (l, 0, j)),
            pl.BlockSpec((None, 1, tn), lambda l, j: (l, 0, j)),
        ],
        out_specs=pl.BlockSpec((None, 8, tn), lambda l, j: (l, 0, j)),
        out_shape=jax.ShapeDtypeStruct((depth, 8, n), F32),
        compiler_params=_params("parallel", "parallel"),
        name="adaln_modulation",
    )(c_rows, ada_w, ada_b.reshape(depth, 1, n))


def _mod_spec(d, layer, which, row=None):
    return pl.BlockSpec((None, None, None, 1, d),
                        lambda *g: (layer, g[0] if row is None else row, which, 0, 0))


def _qkv_kernel(x_ref, sh_ref, sc_ref, gn_ref, w_ref, qg_ref, kg_ref, cos_ref, sin_ref, *out_refs,
                n_q_heads, n_kv_heads, rope, q_scale):
    x = x_ref[...]
    h = (x * _rms_scale(x) * (gn_ref[...] * (1.0 + sc_ref[...])) + sh_ref[...]).astype(BF16)
    if rope:
        cosf, sinf = cos_ref[...], sin_ref[...]
        even = (lax.broadcasted_iota(jnp.int32, cosf.shape, 1) % 2) == 0

    def head_norm_rope(t, gain):
        t = t * _rms_scale(t) * gain
        if rope:
            partner = jnp.where(even, pltpu.roll(t, HEAD_DIM - 1, 1), pltpu.roll(t, 1, 1))
            t = t * cosf + partner * sinf
        return t

    if n_q_heads:
        q_ref, k_ref, v_ref = out_refs
    else:
        k_ref, v_ref = out_refs
    group_w = KV_GROUP * HEAD_DIM
    col = 0
    for g in range(n_q_heads // KV_GROUP):
        acc = jnp.dot(h, w_ref[:, col:col + group_w], preferred_element_type=F32)
        for i in range(KV_GROUP):
            t = head_norm_rope(acc[:, i * HEAD_DIM:(i + 1) * HEAD_DIM], qg_ref[...])
            q_ref[g * KV_GROUP + i] = (t * q_scale).astype(BF16)
        col += group_w
    kw = n_kv_heads * HEAD_DIM
    acc = jnp.dot(h, w_ref[:, col:col + kw], preferred_element_type=F32)
    for i in range(n_kv_heads):
        k_ref[i] = head_norm_rope(acc[:, i * HEAD_DIM:(i + 1) * HEAD_DIM], kg_ref[...]).astype(BF16)
    col += kw
    acc = jnp.dot(h, w_ref[:, col:col + kw], preferred_element_type=F32)
    for i in range(n_kv_heads):
        v_ref[i, :, :HEAD_DIM] = acc[:, i * HEAD_DIM:(i + 1) * HEAD_DIM].astype(BF16)
        v_ref[i, :, HEAD_DIM:] = jnp.ones((acc.shape[0], HEAD_DIM), BF16)


def _qkv_project(x, mod, layer, mod_row, gn, w, qg, kg, cosf, sinf, *, n_q_heads, n_kv_heads, rope, ts):
    b, s, d = x.shape
    ncols = (n_q_heads + 2 * n_kv_heads) * HEAD_DIM
    col_block = (w.shape[1] - ncols) // ncols
    assert col_block * ncols == w.shape[1] - ncols
    head_out = lambda nh, width=HEAD_DIM: pl.BlockSpec((None, nh, ts, width), lambda bi, i: (bi, 0, i, 0))
    head_shape = lambda nh, width=HEAD_DIM: jax.ShapeDtypeStruct((b, nh, s, width), BF16)
    out_specs = [head_out(n_kv_heads), head_out(n_kv_heads, 2 * HEAD_DIM)]
    out_shape = [head_shape(n_kv_heads), head_shape(n_kv_heads, 2 * HEAD_DIM)]
    if n_q_heads:
        out_specs.insert(0, head_out(n_q_heads))
        out_shape.insert(0, head_shape(n_q_heads))
    kernel = functools.partial(_qkv_kernel, n_q_heads=n_q_heads, n_kv_heads=n_kv_heads, rope=rope,
                               q_scale=HEAD_DIM ** -0.5 * LOG2_E)
    return pl.pallas_call(
        kernel,
        grid=(b, s // ts),
        in_specs=[
            pl.BlockSpec((None, ts, d), lambda bi, i: (bi, i, 0)),
            _mod_spec(d, layer, 0, mod_row),
            _mod_spec(d, layer, 1, mod_row),
            pl.BlockSpec((1, d), lambda bi, i: (0, 0)),
            _resident((d, ncols), lambda bi, i: (0, col_block)),
            pl.BlockSpec((1, HEAD_DIM), lambda bi, i: (0, 0)),
            pl.BlockSpec((1, HEAD_DIM), lambda bi, i: (0, 0)),
            pl.BlockSpec((ts, HEAD_DIM), lambda bi, i: (i, 0)),
            pl.BlockSpec((ts, HEAD_DIM), lambda bi, i: (i, 0)),
        ],
        out_specs=out_specs,
        out_shape=out_shape,
        compiler_params=_params("parallel", "parallel"),
        name="qkv_project" if n_q_heads else "kv_project_ctx",
    )(x, mod, mod, gn, w, qg, kg, cosf, sinf)


def _attn_kernel(q_ref, kl_ref, vl_ref, kc_ref, vc_ref, o_ref, *, tkv):
    g, tq, hd = q_ref.shape
    q = q_ref[...].reshape(g * tq, hd)
    chunks = [(kl_ref, vl_ref, i * tkv, tkv) for i in range(kl_ref.shape[0] // tkv)]
    chunks.append((kc_ref, vc_ref, 0, kc_ref.shape[0]))
    m = acc = None
    for k_ref, v_ref, off, n in chunks:
        s = lax.dot_general(q, k_ref[off:off + n, :], (((1,), (1,)), ((), ())), preferred_element_type=F32)
        m_chunk = jnp.max(s, axis=-1, keepdims=True)
        m_new = m_chunk if m is None else jnp.maximum(m, m_chunk)
        pv = jnp.dot(jnp.exp2(s - m_new).astype(BF16), v_ref[off:off + n, :], preferred_element_type=F32)
        acc = pv if m is None else jnp.exp2(m - m_new) * acc + pv
        m = m_new
    o = acc[:, :hd] / acc[:, hd:]
    for i in range(g):
        o_ref[:, i * hd:(i + 1) * hd] = o[i * tq:(i + 1) * tq].astype(o_ref.dtype)


def _attention(q, k_lat, v_lat, k_ctx, v_ctx, *, tq, tkv):
    b, h, s, hd = q.shape
    kvh = h // KV_GROUP
    n_ctx = k_ctx.shape[2]
    kv_spec = lambda n, width=hd: pl.BlockSpec((None, None, n, width), lambda bi, j, i: (bi, j, 0, 0))
    return pl.pallas_call(
        functools.partial(_attn_kernel, tkv=tkv),
        grid=(b, kvh, s // tq),
        in_specs=[
            pl.BlockSpec((None, KV_GROUP, tq, hd), lambda bi, j, i: (bi, j, i, 0)),
            kv_spec(s), kv_spec(s, 2 * hd), kv_spec(n_ctx), kv_spec(n_ctx, 2 * hd),
        ],
        out_specs=pl.BlockSpec((None, tq, KV_GROUP * hd), lambda bi, j, i: (bi, i, j)),
        out_shape=jax.ShapeDtypeStruct((b, s, h * hd), BF16),
        compiler_params=_params("parallel", "parallel", "parallel"),
        name="gqa_attention",
    )(q, k_lat, v_lat, k_ctx, v_ctx)


def _oproj_kernel(a_ref, w_ref, x_ref, g_ref, gn_ref, sc_ref, sh_ref, x1_ref, h_ref):
    y = jnp.dot(a_ref[...], w_ref[...], preferred_element_type=F32)
    x1 = x_ref[...] + g_ref[...] * y
    x1_ref[...] = x1
    h_ref[...] = (x1 * _rms_scale(x1) * (gn_ref[...] * (1.0 + sc_ref[...])) + sh_ref[...]).astype(BF16)


def _out_project(a, w, x, mod, layer, gn, *, tm):
    b, s, d = x.shape
    tile = pl.BlockSpec((None, tm, d), lambda bi, i: (bi, i, 0))
    return pl.pallas_call(
        _oproj_kernel,
        grid=(b, s // tm),
        in_specs=[
            tile,
            _resident((d, d), lambda bi, i: (0, 0)),
            tile,
            _mod_spec(d, layer, 2),
            pl.BlockSpec((1, d), lambda bi, i: (0, 0)),
            _mod_spec(d, layer, 4),
            _mod_spec(d, layer, 3),
        ],
        out_specs=[tile, tile],
        out_shape=[jax.ShapeDtypeStruct((b, s, d), F32), jax.ShapeDtypeStruct((b, s, d), BF16)],
        compiler_params=_params("parallel", "parallel"),
        name="out_project",
    )(a, w, x, mod, gn, mod, mod)


def _mlp_kernel(h_ref, w1_ref, w2_ref, x1_ref, g2_ref, gn_ref, sc_ref, sh_ref, *refs, final):
    c = pl.program_id(2)
    if final:
        out_ref, acc_ref = refs
    else:
        x2_ref, hn_ref, acc_ref, slab_ref = refs

    @pl.when(c == 0)
    def _():
        acc_ref[...] = jnp.zeros_like(acc_ref)

    u = jnp.maximum(jnp.dot(h_ref[...], w1_ref[...], preferred_element_type=F32), 0.0)
    acc_ref[...] += jnp.dot((u * u).astype(BF16), w2_ref[...], preferred_element_type=F32)

    @pl.when(c == pl.num_programs(2) - 1)
    def _():
        x2 = x1_ref[...] + g2_ref[...] * acc_ref[...]
        normed = x2 * _rms_scale(x2)
        if final:
            out_ref[...] = normed * gn_ref[...]
        else:
            x2_ref[...] = x2
            hn = normed * (gn_ref[...] * (1.0 + sc_ref[...])) + sh_ref[...]
            rows = hn.shape[0] // RADIX
            for sl in range(slab_ref.shape[0]):
                lanes = slice(sl * 128, (sl + 1) * 128)
                slab_ref[sl] = hn[:, lanes]
                for j in range(RADIX):
                    hn_ref[j, :, lanes] = slab_ref[sl, pl.ds(j, rows, stride=RADIX), :].astype(BF16)


def _mlp(h, w1, w2, x1, mod, layer, gn_next, next_layer, *, tm, tf):
    b, s, d = x1.shape
    f = w1.shape[2]
    final = next_layer is None
    tile = pl.BlockSpec((None, tm, d), lambda bi, i, c: (bi, i, 0))
    row = pl.BlockSpec((1, d), lambda bi, i, c: (0, 0))
    nl = layer if final else next_layer
    out_shape = [jax.ShapeDtypeStruct((b, s, d), F32)]
    out_specs = [tile]
    scratch = [pltpu.VMEM((tm, d), F32)]
    if not final:
        out_shape.append(jax.ShapeDtypeStruct((b, RADIX, s // RADIX, d), BF16))
        out_specs.append(pl.BlockSpec((None, RADIX, tm // RADIX, d), lambda bi, i, c: (bi, 0, i, 0)))
        scratch.append(pltpu.VMEM((d // 128, tm, 128), F32))
    return pl.pallas_call(
        functools.partial(_mlp_kernel, final=final),
        grid=(b, s // tm, f // tf),
        in_specs=[
            tile,
            pl.BlockSpec((None, d, tf), lambda bi, i, c: (layer, 0, c)),
            pl.BlockSpec((None, tf, d), lambda bi, i, c: (layer, c, 0)),
            tile,
            _mod_spec(d, layer, 5),
            row,
            _mod_spec(d, nl, 1),
            _mod_spec(d, nl, 0),
        ],
        out_specs=out_specs,
        out_shape=out_shape,
        scratch_shapes=scratch,
        compiler_params=_params("parallel", "parallel", "arbitrary"),
        name="mlp_final" if final else "mlp",
    )(h, w1, w2, x1, mod, gn_next, mod, mod)


def _seq_dft_mats(s, tk):
    n = s // RADIX
    k1 = np.arange(n, dtype=np.int64)[:, None]
    m = np.arange(n, dtype=np.int64)[None, :]
    mats = []
    for j in range(RADIX):
        ang = 2.0 * np.pi * (((RADIX * m + j) * k1) % s) / s
        c = (np.cos(ang) / np.sqrt(s)).reshape(n // tk, tk, n)
        sn = (-np.sin(ang) / np.sqrt(s)).reshape(n // tk, tk, n)
        mats.append(np.concatenate([c, sn], axis=1))
    return np.stack(mats).astype(np.float32)


def _chan_dft_mat():
    dg = FOURIER_GROUP_DIM
    idx = np.arange(dg, dtype=np.int64)
    ang = 2.0 * np.pi * ((idx[:, None] * idx[None, :]) % dg) / dg
    return (np.concatenate([np.cos(ang), np.sin(ang)], axis=0) / np.sqrt(dg)).astype(np.float32)


def _fourier_kernel(h0_ref, h1_ref, h2_ref, h3_ref, m_ref, cs_ref, o_ref):
    tk = m_ref.shape[1] // 2
    a, bb = [], []
    for j, h_ref in enumerate((h0_ref, h1_ref, h2_ref, h3_ref)):
        y = jnp.dot(m_ref[j], h_ref[...], preferred_element_type=F32)
        a.append(y[:tk])
        bb.append(y[tk:])
    e0, e1 = a[0] + a[2], a[0] - a[2]
    f0, f1 = bb[0] + bb[2], bb[0] - bb[2]
    o0, o1 = a[1] + a[3], a[1] - a[3]
    q0, q1 = bb[1] + bb[3], bb[1] - bb[3]
    pr = (e0 + o0, e1 + q1, e0 - o0, e1 - q1)
    pi = (f0 + q0, f1 - o1, f0 - q0, f1 + o1)
    dg = FOURIER_GROUP_DIM
    cs = cs_ref[...]
    for k2 in range(RADIX):
        prb, pib = pr[k2].astype(BF16), pi[k2].astype(BF16)
        for gi in range(o_ref.shape[-1] // dg):
            sl = slice(gi * dg, (gi + 1) * dg)
            y = (jnp.dot(prb[:, sl], cs[:dg], preferred_element_type=F32)
                 + jnp.dot(pib[:, sl], cs[dg:], preferred_element_type=F32))
            o_ref[k2, :, sl] = y.astype(o_ref.dtype)


def _fourier_mix(h, *, tk, tn):
    b, _, n, d = h.shape
    s = n * RADIX
    mats = jnp.asarray(_seq_dft_mats(s, tk)).astype(BF16)
    cs = jnp.asarray(_chan_dft_mat()).astype(BF16)
    h_spec = lambda j: pl.BlockSpec((None, None, n, tn), lambda bi, c, t: (bi, j, 0, c))
    y = pl.pallas_call(
        _fourier_kernel,
        grid=(b, d // tn, n // tk),
        in_specs=[
            h_spec(0), h_spec(1), h_spec(2), h_spec(3),
            pl.BlockSpec((RADIX, None, 2 * tk, n), lambda bi, c, t: (0, t, 0, 0)),
            pl.BlockSpec((2 * FOURIER_GROUP_DIM, FOURIER_GROUP_DIM), lambda bi, c, t: (0, 0)),
        ],
        out_specs=pl.BlockSpec((None, RADIX, tk, tn), lambda bi, c, t: (bi, 0, t, c)),
        out_shape=jax.ShapeDtypeStruct((b, RADIX, n, d), BF16),
        compiler_params=_params("parallel", "parallel", "parallel"),
        name="fourier_mix",
    )(h, h, h, h, mats, cs)
    return y.reshape(b, s, d)


def _rope_tables(s):
    rows = s // GRID_W
    row = jnp.broadcast_to(jnp.arange(rows)[:, None], (rows, GRID_W)).reshape(-1)
    col = jnp.broadcast_to(jnp.arange(GRID_W)[None, :], (rows, GRID_W)).reshape(-1)
    n_freq = HEAD_DIM // 4
    inv_freq = ROPE_THETA ** (-jnp.arange(n_freq, dtype=F32) / n_freq)
    ang = jnp.concatenate([row.astype(F32)[:, None] * inv_freq, col.astype(F32)[:, None] * inv_freq], axis=-1)
    cos, sin = jnp.cos(ang), jnp.sin(ang)
    cosf = jnp.repeat(cos, 2, axis=-1)
    sinf = jnp.stack([-sin, sin], axis=-1).reshape(s, HEAD_DIM)
    return cosf, sinf


def _tiles(s, d, f):
    return dict(
        ts=min(512, s), tq=min(256, s), tkv=min(512, s), tm_o=min(256, s), tm=min(512, s), tf=min(1024, f),
        tk=min(256, s // RADIX), tn=min(512, d),
    )


def kernel(x, c, ctx, c_ctx, ada_w, ada_b, norm_mix_g, norm_mlp_g, attn_w_qkv, attn_q_norm_g, attn_k_norm_g,
           attn_w_o, fourier_w_o, mlp_w1, mlp_w2, final_norm_g):
    b, s, d = x.shape
    depth = ada_w.shape[0]
    assert depth == 2 and d % (KV_GROUP * HEAD_DIM) == 0 and s % GRID_W == 0 and b <= 4
    n_heads = d // HEAD_DIM
    n_kv = n_heads // KV_GROUP
    f = mlp_w1.shape[2]
    t = _tiles(s, d, f)

    c_rows = jnp.zeros((8, d), F32).at[:b].set(c).at[4].set(c_ctx)
    mod = _modulation(c_rows, ada_w, ada_b, tn=min(1024, N_MOD * d)).reshape(depth, 8, N_MOD, 1, d)

    w_qkv = attn_w_qkv[0].astype(BF16)
    w_ao = attn_w_o[0].astype(BF16)
    w_fo = fourier_w_o[0].astype(BF16)
    w1 = mlp_w1.astype(BF16)
    w2 = mlp_w2.astype(BF16)
    gmix = norm_mix_g.reshape(depth, 1, d)
    gmlp = norm_mlp_g.reshape(depth, 1, d)
    qg = attn_q_norm_g[0].reshape(1, HEAD_DIM)
    kg = attn_k_norm_g[0].reshape(1, HEAD_DIM)
    cosf, sinf = _rope_tables(s)

    q, k_lat, v_lat = _qkv_project(x, mod, 0, None, gmix[0], w_qkv, qg, kg, cosf, sinf,
                                   n_q_heads=n_heads, n_kv_heads=n_kv, rope=True, ts=t["ts"])
    n_ctx = ctx.shape[1]
    k_ctx, v_ctx = _qkv_project(ctx, mod, 0, 4, gmix[0], w_qkv, qg, kg, cosf, sinf,
                                n_q_heads=0, n_kv_heads=n_kv, rope=False, ts=n_ctx)
    o = _attention(q, k_lat, v_lat, k_ctx, v_ctx, tq=t["tq"], tkv=t["tkv"])
    x1, h = _out_project(o, w_ao, x, mod, 0, gmlp[0], tm=t["tm_o"])
    x2, h = _mlp(h, w1, w2, x1, mod, 0, gmix[1], 1, tm=t["tm"], tf=t["tf"])

    y = _fourier_mix(h, tk=t["tk"], tn=t["tn"])
    x1, h = _out_project(y, w_fo, x2, mod, 1, gmlp[1], tm=t["tm_o"])
    (out,) = _mlp(h, w1, w2, x1, mod, 1, final_norm_g.reshape(1, d), None, tm=t["tm"], tf=t["tf"])
    return out
```

```python
import functools

import numpy as np
import jax
import jax.numpy as jnp
from jax import lax
from jax.experimental import pallas as pl
from jax.experimental.pallas import tpu as pltpu

HEAD_DIM = 128
KV_GROUP = 4
GRID_W = 64
ROPE_THETA = 10000.0
FOURIER_GROUP_DIM = 256
N_MOD = 6
EPS = 1e-6
RADIX = 4
LOG2_E = 1.4426950408889634
ROW_SUB = 256
LANES = 128
VMEM_LIMIT_BYTES = 56 * 1024 * 1024

F32 = jnp.float32
BF16 = jnp.bfloat16


def _params(*sem):
    return pltpu.CompilerParams(dimension_semantics=sem, vmem_limit_bytes=VMEM_LIMIT_BYTES)


def _resident(shape, index_map):
    return pl.BlockSpec(shape, index_map, pipeline_mode=pl.Buffered(1))


def _rms_scale(x):
    return lax.rsqrt(jnp.mean(x * x, axis=-1, keepdims=True) + EPS)


def _mod_kernel(c_ref, w_ref, b_ref, o_ref):
    c = c_ref[...]
    s = c / (1.0 + jnp.exp(-c))
    o_ref[...] = jnp.dot(s, w_ref[...], preferred_element_type=F32) + b_ref[...]


def _modulation(c_rows, ada_w, ada_b, tn):
    depth, d, n = ada_w.shape
    return pl.pallas_call(
        _mod_kernel,
        grid=(depth, n // tn),
        in_specs=[
            pl.BlockSpec((8, d), lambda l, j: (0, 0)),
            pl.BlockSpec((None, d, tn), lambda l, j: (l, 0, j)),
            pl.BlockSpec((None, 1, tn), lambda l, j: (l, 0, j)),
        ],
        out_specs=pl.BlockSpec((None, 8, tn), lambda l, j: (l, 0, j)),
        out_shape=jax.ShapeDtypeStruct((depth, 8, n), F32),
        compiler_params=_params("parallel", "parallel"),
        name="adaln_modulation",
    )(c_rows, ada_w, ada_b.reshape(depth, 1, n))


def _mod_spec(d, layer, which, row=None):
    return pl.BlockSpec((None, None, None, 1, d),
                        lambda *g: (layer, g[0] if row is None else row, which, 0, 0))


def _qkv_kernel(x_ref, sh_ref, sc_ref, gn_ref, w_ref, qg_ref, kg_ref, cos_ref, sin_ref, *out_refs,
                n_q_heads, n_kv_heads, rope, q_scale, sub):
    if n_q_heads:
        q_ref, k_ref, v_ref = out_refs
    else:
        k_ref, v_ref = out_refs
    group_w = KV_GROUP * HEAD_DIM
    kw = n_kv_heads * HEAD_DIM
    gain_row = gn_ref[...] * (1.0 + sc_ref[...])
    for r in range(x_ref.shape[0] // sub):
        rows = slice(r * sub, (r + 1) * sub)
        x = x_ref[rows, :]
        h = (x * _rms_scale(x) * gain_row + sh_ref[...]).astype(BF16)
        if rope:
            cosf, sinf = cos_ref[rows, :], sin_ref[rows, :]
            even = (lax.broadcasted_iota(jnp.int32, cosf.shape, 1) % 2) == 0

        def head_norm_rope(t, gain):
            t = t * _rms_scale(t) * gain
            if rope:
                partner = jnp.where(even, pltpu.roll(t, HEAD_DIM - 1, 1), pltpu.roll(t, 1, 1))
                t = t * cosf + partner * sinf
            return t

        col = 0
        for g in range(n_q_heads // KV_GROUP):
            acc = jnp.dot(h, w_ref[:, col:col + group_w], preferred_element_type=F32)
            for i in range(KV_GROUP):
                t = head_norm_rope(acc[:, i * HEAD_DIM:(i + 1) * HEAD_DIM], qg_ref[...])
                q_ref[g * KV_GROUP + i, rows, :] = (t * q_scale).astype(BF16)
            col += group_w
        acc = jnp.dot(h, w_ref[:, col:col + kw], preferred_element_type=F32)
        for i in range(n_kv_heads):
            k_ref[i, rows, :] = head_norm_rope(acc[:, i * HEAD_DIM:(i + 1) * HEAD_DIM], kg_ref[...]).astype(BF16)
        col += kw
        acc = jnp.dot(h, w_ref[:, col:col + kw], preferred_element_type=F32)
        for i in range(n_kv_heads):
            v_ref[i, rows, :HEAD_DIM] = acc[:, i * HEAD_DIM:(i + 1) * HEAD_DIM].astype(BF16)
            v_ref[i, rows, HEAD_DIM:] = jnp.ones((sub, HEAD_DIM), BF16)


def _qkv_project(x, mod, layer, mod_row, gn, w, qg, kg, cosf, sinf, *, n_q_heads, n_kv_heads, rope, ts):
    b, s, d = x.shape
    ncols = (n_q_heads + 2 * n_kv_heads) * HEAD_DIM
    col_block = (w.shape[1] - ncols) // ncols
    assert col_block * ncols == w.shape[1] - ncols
    head_out = lambda nh, width=HEAD_DIM: pl.BlockSpec((None, nh, ts, width), lambda bi, i: (bi, 0, i, 0))
    head_shape = lambda nh, width=HEAD_DIM: jax.ShapeDtypeStruct((b, nh, s, width), BF16)
    out_specs = [head_out(n_kv_heads), head_out(n_kv_heads, 2 * HEAD_DIM)]
    out_shape = [head_shape(n_kv_heads), head_shape(n_kv_heads, 2 * HEAD_DIM)]
    if n_q_heads:
        out_specs.insert(0, head_out(n_q_heads))
        out_shape.insert(0, head_shape(n_q_heads))
    kernel = functools.partial(_qkv_kernel, n_q_heads=n_q_heads, n_kv_heads=n_kv_heads, rope=rope,
                               q_scale=HEAD_DIM ** -0.5 * LOG2_E, sub=min(ROW_SUB, ts))
    return pl.pallas_call(
        kernel,
        grid=(b, s // ts),
        in_specs=[
            pl.BlockSpec((None, ts, d), lambda bi, i: (bi, i, 0)),
            _mod_spec(d, layer, 0, mod_row),
            _mod_spec(d, layer, 1, mod_row),
            pl.BlockSpec((1, d), lambda bi, i: (0, 0)),
            _resident((d, ncols), lambda bi, i: (0, col_block)),
            pl.BlockSpec((1, HEAD_DIM), lambda bi, i: (0, 0)),
            pl.BlockSpec((1, HEAD_DIM), lambda bi, i: (0, 0)),
            pl.BlockSpec((ts, HEAD_DIM), lambda bi, i: (i, 0)),
            pl.BlockSpec((ts, HEAD_DIM), lambda bi, i: (i, 0)),
        ],
        out_specs=out_specs,
        out_shape=out_shape,
        compiler_params=_params("parallel", "parallel"),
        name="qkv_project" if n_q_heads else "kv_project_ctx",
    )(x, mod, mod, gn, w, qg, kg, cosf, sinf)


def _attn_kernel(q_ref, kl_ref, vl_ref, kc_ref, vc_ref, o_ref, *, tkv):
    g, tq, hd = q_ref.shape
    q = q_ref[...].reshape(g * tq, hd)
    chunks = [(kl_ref, vl_ref, i * tkv, tkv) for i in range(kl_ref.shape[0] // tkv)]
    chunks.append((kc_ref, vc_ref, 0, kc_ref.shape[0]))
    m = acc = None
    for k_ref, v_ref, off, n in chunks:
        s = lax.dot_general(q, k_ref[off:off + n, :], (((1,), (1,)), ((), ())), preferred_element_type=F32)
        m_chunk = jnp.max(s, axis=-1, keepdims=True)
        m_new = m_chunk if m is None else jnp.maximum(m, m_chunk)
        pv = jnp.dot(jnp.exp2(s - m_new).astype(BF16), v_ref[off:off + n, :], preferred_element_type=F32)
        acc = pv if m is None else jnp.exp2(m - m_new) * acc + pv
        m = m_new
    o = acc[:, :hd] / acc[:, hd:]
    for i in range(g):
        o_ref[:, i * hd:(i + 1) * hd] = o[i * tq:(i + 1) * tq].astype(o_ref.dtype)


def _attention(q, k_lat, v_lat, k_ctx, v_ctx, *, tq, tkv):
    b, h, s, hd = q.shape
    kvh = h // KV_GROUP
    n_ctx = k_ctx.shape[2]
    kv_spec = lambda n, width=hd: pl.BlockSpec((None, None, n, width), lambda bi, j, i: (bi, j, 0, 0))
    return pl.pallas_call(
        functools.partial(_attn_kernel, tkv=tkv),
        grid=(b, kvh, s // tq),
        in_specs=[
            pl.BlockSpec((None, KV_GROUP, tq, hd), lambda bi, j, i: (bi, j, i, 0)),
            kv_spec(s), kv_spec(s, 2 * hd), kv_spec(n_ctx), kv_spec(n_ctx, 2 * hd),
        ],
        out_specs=pl.BlockSpec((None, tq, KV_GROUP * hd), lambda bi, j, i: (bi, i, j)),
        out_shape=jax.ShapeDtypeStruct((b, s, h * hd), BF16),
        compiler_params=_params("parallel", "parallel", "parallel"),
        name="gqa_attention",
    )(q, k_lat, v_lat, k_ctx, v_ctx)


def _oproj_kernel(a_ref, w_ref, x_ref, g_ref, gn_ref, sc_ref, sh_ref, x1_ref, h_ref, *, sub):
    gain_row = gn_ref[...] * (1.0 + sc_ref[...])
    for r in range(a_ref.shape[0] // sub):
        rows = slice(r * sub, (r + 1) * sub)
        y = jnp.dot(a_ref[rows, :], w_ref[...], preferred_element_type=F32)
        x1 = x_ref[rows, :] + g_ref[...] * y
        x1_ref[rows, :] = x1
        h_ref[rows, :] = (x1 * _rms_scale(x1) * gain_row + sh_ref[...]).astype(BF16)


def _out_project(a, w, x, mod, layer, gn, *, tm):
    b, s, d = x.shape
    tile = pl.BlockSpec((None, tm, d), lambda bi, i: (bi, i, 0))
    return pl.pallas_call(
        functools.partial(_oproj_kernel, sub=min(ROW_SUB, tm)),
        grid=(b, s // tm),
        in_specs=[
            tile,
            _resident((d, d), lambda bi, i: (0, 0)),
            tile,
            _mod_spec(d, layer, 2),
            pl.BlockSpec((1, d), lambda bi, i: (0, 0)),
            _mod_spec(d, layer, 4),
            _mod_spec(d, layer, 3),
        ],
        out_specs=[tile, tile],
        out_shape=[jax.ShapeDtypeStruct((b, s, d), F32), jax.ShapeDtypeStruct((b, s, d), BF16)],
        compiler_params=_params("parallel", "parallel"),
        name="out_project",
    )(a, w, x, mod, gn, mod, mod)


def _mlp_kernel(h_ref, w1_ref, w2_ref, x1_ref, g2_ref, gn_ref, sc_ref, sh_ref, *refs, final, sub):
    c = pl.program_id(2)
    last = pl.num_programs(2) - 1
    if final:
        out_ref, acc_ref = refs
    else:
        x2_ref, hn_ref, acc_ref, slab_ref = refs

    def chunk_sum(rows):
        u = jnp.maximum(jnp.dot(h_ref[rows, :], w1_ref[...], preferred_element_type=F32), 0.0)
        return jnp.dot((u * u).astype(BF16), w2_ref[...], preferred_element_type=F32)

    everything = slice(None)

    @pl.when(c == 0)
    def _():
        acc_ref[...] = chunk_sum(everything)

    @pl.when(jnp.logical_and(c > 0, c < last))
    def _():
        acc_ref[...] += chunk_sum(everything)

    @pl.when(c == last)
    def _():
        for r in range(h_ref.shape[0] // sub):
            rows = slice(r * sub, (r + 1) * sub)
            x2 = x1_ref[rows, :] + g2_ref[...] * (acc_ref[rows, :] + chunk_sum(rows))
            normed = x2 * _rms_scale(x2)
            if final:
                out_ref[rows, :] = normed * gn_ref[...]
            else:
                x2_ref[rows, :] = x2
                hn = normed * (gn_ref[...] * (1.0 + sc_ref[...])) + sh_ref[...]
                prow = slice(r * (sub // RADIX), (r + 1) * (sub // RADIX))
                for sl in range(slab_ref.shape[0]):
                    lanes = slice(sl * LANES, (sl + 1) * LANES)
                    slab_ref[sl, rows, :] = hn[:, lanes]
                    for j in range(RADIX):
                        hn_ref[j, prow, lanes] = slab_ref[
                            sl, pl.ds(r * sub + j, sub // RADIX, stride=RADIX), :].astype(BF16)


def _mlp(h, w1, w2, x1, mod, layer, gn_next, next_layer, *, tm, tf):
    b, s, d = x1.shape
    f = w1.shape[2]
    assert f // tf >= 2
    final = next_layer is None
    tile = pl.BlockSpec((None, tm, d), lambda bi, i, c: (bi, i, 0))
    row = pl.BlockSpec((1, d), lambda bi, i, c: (0, 0))
    nl = layer if final else next_layer
    out_shape = [jax.ShapeDtypeStruct((b, s, d), F32)]
    out_specs = [tile]
    scratch = [pltpu.VMEM((tm, d), F32)]
    if not final:
        out_shape.append(jax.ShapeDtypeStruct((b, RADIX, s // RADIX, d), BF16))
        out_specs.append(pl.BlockSpec((None, RADIX, tm // RADIX, d), lambda bi, i, c: (bi, 0, i, 0)))
        scratch.append(pltpu.VMEM((d // LANES, tm, LANES), F32))
    return pl.pallas_call(
        functools.partial(_mlp_kernel, final=final, sub=min(ROW_SUB, tm)),
        grid=(b, s // tm, f // tf),
        in_specs=[
            tile,
            pl.BlockSpec((None, d, tf), lambda bi, i, c: (layer, 0, c)),
            pl.BlockSpec((None, tf, d), lambda bi, i, c: (layer, c, 0)),
            tile,
            _mod_spec(d, layer, 5),
            row,
            _mod_spec(d, nl, 1),
            _mod_spec(d, nl, 0),
        ],
        out_specs=out_specs,
        out_shape=out_shape,
        scratch_shapes=scratch,
        compiler_params=_params("parallel", "parallel", "arbitrary"),
        name="mlp_final" if final else "mlp",
    )(h, w1, w2, x1, mod, gn_next, mod, mod)


def _seq_dft_mats(s, tk):
    n = s // RADIX
    k1 = np.arange(n, dtype=np.int64)[:, None]
    m = np.arange(n, dtype=np.int64)[None, :]
    mats = []
    for j in range(RADIX):
        ang = 2.0 * np.pi * (((RADIX * m + j) * k1) % s) / s
        c = (np.cos(ang) / np.sqrt(s)).reshape(n // tk, tk, n)
        sn = (-np.sin(ang) / np.sqrt(s)).reshape(n // tk, tk, n)
        mats.append(np.concatenate([c, sn], axis=1))
    return np.stack(mats).astype(np.float32)


def _chan_dft_mat():
    dg = FOURIER_GROUP_DIM
    idx = np.arange(dg, dtype=np.int64)
    ang = 2.0 * np.pi * ((idx[:, None] * idx[None, :]) % dg) / dg
    return (np.concatenate([np.cos(ang), np.sin(ang)], axis=0) / np.sqrt(dg)).astype(np.float32)


def _fourier_kernel(h0_ref, h1_ref, h2_ref, h3_ref, m_ref, cs_ref, o_ref):
    nsub, tk = m_ref.shape[1], m_ref.shape[2] // 2
    dg = FOURIER_GROUP_DIM
    cs = cs_ref[...]
    for r in range(nsub):
        a, bb = [], []
        for j, h_ref in enumerate((h0_ref, h1_ref, h2_ref, h3_ref)):
            y = jnp.dot(m_ref[j, r], h_ref[...], preferred_element_type=F32)
            a.append(y[:tk])
            bb.append(y[tk:])
        e0, e1 = a[0] + a[2], a[0] - a[2]
        f0, f1 = bb[0] + bb[2], bb[0] - bb[2]
        o0, o1 = a[1] + a[3], a[1] - a[3]
        q0, q1 = bb[1] + bb[3], bb[1] - bb[3]
        pr = (e0 + o0, e1 + q1, e0 - o0, e1 - q1)
        pi = (f0 + q0, f1 - o1, f0 - q0, f1 + o1)
        rows = slice(r * tk, (r + 1) * tk)
        for k2 in range(RADIX):
            prb, pib = pr[k2].astype(BF16), pi[k2].astype(BF16)
            for gi in range(o_ref.shape[-1] // dg):
                sl = slice(gi * dg, (gi + 1) * dg)
                y = (jnp.dot(prb[:, sl], cs[:dg], preferred_element_type=F32)
                     + jnp.dot(pib[:, sl], cs[dg:], preferred_element_type=F32))
                o_ref[k2, rows, sl] = y.astype(o_ref.dtype)


def _fourier_mix(h, *, tk, nsub, tn):
    b, _, n, d = h.shape
    s = n * RADIX
    mats = jnp.asarray(_seq_dft_mats(s, tk)).astype(BF16)
    cs = jnp.asarray(_chan_dft_mat()).astype(BF16)
    h_spec = lambda j: pl.BlockSpec((None, None, n, tn), lambda bi, c, t: (bi, j, 0, c))
    y = pl.pallas_call(
        _fourier_kernel,
        grid=(b, d // tn, n // (tk * nsub)),
        in_specs=[
            h_spec(0), h_spec(1), h_spec(2), h_spec(3),
            pl.BlockSpec((RADIX, nsub, 2 * tk, n), lambda bi, c, t: (0, t, 0, 0)),
            pl.BlockSpec((2 * FOURIER_GROUP_DIM, FOURIER_GROUP_DIM), lambda bi, c, t: (0, 0)),
        ],
        out_specs=pl.BlockSpec((None, RADIX, tk * nsub, tn), lambda bi, c, t: (bi, 0, t, c)),
        out_shape=jax.ShapeDtypeStruct((b, RADIX, n, d), BF16),
        compiler_params=_params("parallel", "parallel", "parallel"),
        name="fourier_mix",
    )(h, h, h, h, mats, cs)
    return y.reshape(b, s, d)


def _rope_tables(s):
    rows = s // GRID_W
    row = jnp.broadcast_to(jnp.arange(rows)[:, None], (rows, GRID_W)).reshape(-1)
    col = jnp.broadcast_to(jnp.arange(GRID_W)[None, :], (rows, GRID_W)).reshape(-1)
    n_freq = HEAD_DIM // 4
    inv_freq = ROPE_THETA ** (-jnp.arange(n_freq, dtype=F32) / n_freq)
    ang = jnp.concatenate([row.astype(F32)[:, None] * inv_freq, col.astype(F32)[:, None] * inv_freq], axis=-1)
    cos, sin = jnp.cos(ang), jnp.sin(ang)
    cosf = jnp.repeat(cos, 2, axis=-1)
    sinf = jnp.stack([-sin, sin], axis=-1).reshape(s, HEAD_DIM)
    return cosf, sinf


def _tiles(s, d, f):
    return dict(
        ts=min(1024, s), tq=min(512, s), tkv=min(512, s), tm_o=min(512, s), tm=min(512, s), tf=min(1024, f),
        tk=min(ROW_SUB, s // RADIX), nsub=max(1, min(2, s // RADIX // ROW_SUB)), tn=min(512, d),
    )


def kernel(x, c, ctx, c_ctx, ada_w, ada_b, norm_mix_g, norm_mlp_g, attn_w_qkv, attn_q_norm_g, attn_k_norm_g,
           attn_w_o, fourier_w_o, mlp_w1, mlp_w2, final_norm_g):
    b, s, d = x.shape
    depth = ada_w.shape[0]
    assert depth == 2 and d % (KV_GROUP * HEAD_DIM) == 0 and s % GRID_W == 0 and b <= 4
    n_heads = d // HEAD_DIM
    n_kv = n_heads // KV_GROUP
    f = mlp_w1.shape[2]
    t = _tiles(s, d, f)

    c_rows = jnp.zeros((8, d), F32).at[:b].set(c).at[4].set(c_ctx)
    mod = _modulation(c_rows, ada_w, ada_b, tn=min(1024, N_MOD * d)).reshape(depth, 8, N_MOD, 1, d)

    w_qkv = attn_w_qkv[0].astype(BF16)
    w_ao = attn_w_o[0].astype(BF16)
    w_fo = fourier_w_o[0].astype(BF16)
    w1 = mlp_w1.astype(BF16)
    w2 = mlp_w2.astype(BF16)
    gmix = norm_mix_g.reshape(depth, 1, d)
    gmlp = norm_mlp_g.reshape(depth, 1, d)
    qg = attn_q_norm_g[0].reshape(1, HEAD_DIM)
    kg = attn_k_norm_g[0].reshape(1, HEAD_DIM)
    cosf, sinf = _rope_tables(s)

    q, k_lat, v_lat = _qkv_project(x, mod, 0, None, gmix[0], w_qkv, qg, kg, cosf, sinf,
                                   n_q_heads=n_heads, n_kv_heads=n_kv, rope=True, ts=t["ts"])
    n_ctx = ctx.shape[1]
    k_ctx, v_ctx = _qkv_project(ctx, mod, 0, 4, gmix[0], w_qkv, qg, kg, cosf, sinf,
                                n_q_heads=0, n_kv_heads=n_kv, rope=False, ts=n_ctx)
    o = _attention(q, k_lat, v_lat, k_ctx, v_ctx, tq=t["tq"], tkv=t["tkv"])
    x1, h = _out_project(o, w_ao, x, mod, 0, gmlp[0], tm=t["tm_o"])
    x2, h = _mlp(h, w1, w2, x1, mod, 0, gmix[1], 1, tm=t["tm"], tf=t["tf"])

    y = _fourier_mix(h, tk=t["tk"], nsub=t["nsub"], tn=t["tn"])
    x1, h = _out_project(y, w_fo, x2, mod, 1, gmlp[1], tm=t["tm_o"])
    (out,) = _mlp(h, w1, w2, x1, mod, 1, final_norm_g.reshape(1, d), None, tm=t["tm"], tf=t["tf"])
    return out
```

```python
import functools

import numpy as np
import jax
import jax.numpy as jnp
from jax import lax
from jax.experimental import pallas as pl
from jax.experimental.pallas import tpu as pltpu

HEAD_DIM = 128
KV_GROUP = 4
GRID_W = 64
ROPE_THETA = 10000.0
FOURIER_GROUP_DIM = 256
N_MOD = 6
EPS = 1e-6
RADIX = 4
LOG2_E = 1.4426950408889634
ROW_SUB = 256
LANES = 128
BF16_SUBLANES = 16
VMEM_LIMIT_BYTES = 56 * 1024 * 1024

F32 = jnp.float32
BF16 = jnp.bfloat16


def _params(*sem):
    return pltpu.CompilerParams(dimension_semantics=sem, vmem_limit_bytes=VMEM_LIMIT_BYTES)


def _resident(shape, index_map):
    return pl.BlockSpec(shape, index_map, pipeline_mode=pl.Buffered(1))


def _rms_scale(x):
    return lax.rsqrt(jnp.mean(x * x, axis=-1, keepdims=True) + EPS)


def _mod_kernel(c_ref, w_ref, b_ref, o_ref):
    c = c_ref[...]
    s = c / (1.0 + jnp.exp(-c))
    o_ref[...] = jnp.dot(s, w_ref[...], preferred_element_type=F32) + b_ref[...]


def _modulation(c_rows, ada_w, ada_b, tn):
    depth, d, n = ada_w.shape
    return pl.pallas_call(
        _mod_kernel,
        grid=(depth, n // tn),
        in_specs=[
            pl.BlockSpec((8, d), lambda l, j: (0, 0)),
            pl.BlockSpec((None, d, tn), lambda l, j: (l, 0, j)),
            pl.BlockSpec((None, 1, tn), lambda l, j: (l, 0, j)),
        ],
        out_specs=pl.BlockSpec((None, 8, tn), lambda l, j: (l, 0, j)),
        out_shape=jax.ShapeDtypeStruct((depth, 8, n), F32),
        compiler_params=_params("parallel", "parallel"),
        name="adaln_modulation",
    )(c_rows, ada_w, ada_b.reshape(depth, 1, n))


def _mod_spec(d, layer, which, row=None):
    return pl.BlockSpec((None, None, None, 1, d),
                        lambda *g: (layer, g[0] if row is None else row, which, 0, 0))


def _qkv_kernel(x_ref, sh_ref, sc_ref, gn_ref, w_ref, qg_ref, kg_ref, cos_ref, sin_ref, *out_refs,
                n_q_heads, n_kv_heads, rope, q_scale, sub):
    if n_q_heads:
        q_ref, k_ref, v_ref = out_refs
    else:
        k_ref, v_ref = out_refs
    group_w = KV_GROUP * HEAD_DIM
    kw = n_kv_heads * HEAD_DIM
    gain_row = gn_ref[...] * (1.0 + sc_ref[...])
    for r in range(x_ref.shape[0] // sub):
        rows = slice(r * sub, (r + 1) * sub)
        x = x_ref[rows, :]
        h = (x * _rms_scale(x) * gain_row + sh_ref[...]).astype(BF16)
        if rope:
            cosf, sinf = cos_ref[rows, :], sin_ref[rows, :]
            even = (lax.broadcasted_iota(jnp.int32, cosf.shape, 1) % 2) == 0

        def head_norm_rope(t, gain):
            t = t * _rms_scale(t) * gain
            if rope:
                partner = jnp.where(even, pltpu.roll(t, HEAD_DIM - 1, 1), pltpu.roll(t, 1, 1))
                t = t * cosf + partner * sinf
            return t

        col = 0
        for g in range(n_q_heads // KV_GROUP):
            acc = jnp.dot(h, w_ref[:, col:col + group_w], preferred_element_type=F32)
            for i in range(KV_GROUP):
                t = head_norm_rope(acc[:, i * HEAD_DIM:(i + 1) * HEAD_DIM], qg_ref[...])
                q_ref[g * KV_GROUP + i, rows, :] = (t * q_scale).astype(BF16)
            col += group_w
        acc = jnp.dot(h, w_ref[:, col:col + kw], preferred_element_type=F32)
        for i in range(n_kv_heads):
            k_ref[i, rows, :] = head_norm_rope(acc[:, i * HEAD_DIM:(i + 1) * HEAD_DIM], kg_ref[...]).astype(BF16)
        col += kw
        acc = jnp.dot(h, w_ref[:, col:col + kw], preferred_element_type=F32)
        for i in range(n_kv_heads):
            v_ref[i, rows, :HEAD_DIM] = acc[:, i * HEAD_DIM:(i + 1) * HEAD_DIM].astype(BF16)
            v_ref[i, rows, HEAD_DIM:] = jnp.ones((sub, HEAD_DIM), BF16)


def _qkv_project(x, mod, layer, mod_row, gn, w, qg, kg, cosf, sinf, *, n_q_heads, n_kv_heads, rope, ts):
    b, s, d = x.shape
    ncols = (n_q_heads + 2 * n_kv_heads) * HEAD_DIM
    col_block = (w.shape[1] - ncols) // ncols
    assert col_block * ncols == w.shape[1] - ncols
    head_out = lambda nh, width=HEAD_DIM: pl.BlockSpec((None, nh, ts, width), lambda bi, i: (bi, 0, i, 0))
    head_shape = lambda nh, width=HEAD_DIM: jax.ShapeDtypeStruct((b, nh, s, width), BF16)
    out_specs = [head_out(n_kv_heads), head_out(n_kv_heads, 2 * HEAD_DIM)]
    out_shape = [head_shape(n_kv_heads), head_shape(n_kv_heads, 2 * HEAD_DIM)]
    if n_q_heads:
        out_specs.insert(0, head_out(n_q_heads))
        out_shape.insert(0, head_shape(n_q_heads))
    kernel = functools.partial(_qkv_kernel, n_q_heads=n_q_heads, n_kv_heads=n_kv_heads, rope=rope,
                               q_scale=HEAD_DIM ** -0.5 * LOG2_E, sub=min(ROW_SUB, ts))
    return pl.pallas_call(
        kernel,
        grid=(b, s // ts),
        in_specs=[
            pl.BlockSpec((None, ts, d), lambda bi, i: (bi, i, 0)),
            _mod_spec(d, layer, 0, mod_row),
            _mod_spec(d, layer, 1, mod_row),
            pl.BlockSpec((1, d), lambda bi, i: (0, 0)),
            _resident((d, ncols), lambda bi, i: (0, col_block)),
            pl.BlockSpec((1, HEAD_DIM), lambda bi, i: (0, 0)),
            pl.BlockSpec((1, HEAD_DIM), lambda bi, i: (0, 0)),
            pl.BlockSpec((ts, HEAD_DIM), lambda bi, i: (i, 0)),
            pl.BlockSpec((ts, HEAD_DIM), lambda bi, i: (i, 0)),
        ],
        out_specs=out_specs,
        out_shape=out_shape,
        compiler_params=_params("parallel", "parallel"),
        name="qkv_project" if n_q_heads else "kv_project_ctx",
    )(x, mod, mod, gn, w, qg, kg, cosf, sinf)


def _attn_kernel(q_ref, kl_ref, vl_ref, kc_ref, vc_ref, *refs, tkv, n_cast):
    cast_in, o_ref, cast_out = refs[:n_cast], refs[n_cast], refs[n_cast + 1:]
    for src, dst in zip(cast_in, cast_out):
        dst[...] = src[...].astype(dst.dtype)
    g, tq, hd = q_ref.shape
    q = q_ref[...].reshape(g * tq, hd)
    chunks = [(kl_ref, vl_ref, i * tkv, tkv) for i in range(kl_ref.shape[0] // tkv)]
    chunks.append((kc_ref, vc_ref, 0, kc_ref.shape[0]))
    m = acc = None
    for k_ref, v_ref, off, n in chunks:
        s = lax.dot_general(q, k_ref[off:off + n, :], (((1,), (1,)), ((), ())), preferred_element_type=F32)
        m_chunk = jnp.max(s, axis=-1, keepdims=True)
        m_new = m_chunk if m is None else jnp.maximum(m, m_chunk)
        pv = jnp.dot(jnp.exp2(s - m_new).astype(BF16), v_ref[off:off + n, :], preferred_element_type=F32)
        acc = pv if m is None else jnp.exp2(m - m_new) * acc + pv
        m = m_new
    o = acc[:, :hd] / acc[:, hd:]
    for i in range(g):
        o_ref[:, i * hd:(i + 1) * hd] = o[i * tq:(i + 1) * tq].astype(o_ref.dtype)


def _attention(q, k_lat, v_lat, k_ctx, v_ctx, weights, *, tq, tkv):
    b, h, s, hd = q.shape
    kvh = h // KV_GROUP
    n_ctx = k_ctx.shape[2]
    n_i = s // tq
    steps = b * kvh * n_i
    kv_spec = lambda n, width=hd: pl.BlockSpec((None, None, n, width), lambda bi, j, i: (bi, j, 0, 0))
    sliced = [w.reshape(steps, -1, w.shape[-1]) for w in weights]
    for w in sliced:
        assert w.shape[1] % BF16_SUBLANES == 0
    cast_specs = [pl.BlockSpec((None,) + w.shape[1:], lambda bi, j, i: ((bi * kvh + j) * n_i + i, 0, 0))
                  for w in sliced]
    out = pl.pallas_call(
        functools.partial(_attn_kernel, tkv=tkv, n_cast=len(sliced)),
        grid=(b, kvh, n_i),
        in_specs=[
            pl.BlockSpec((None, KV_GROUP, tq, hd), lambda bi, j, i: (bi, j, i, 0)),
            kv_spec(s), kv_spec(s, 2 * hd), kv_spec(n_ctx), kv_spec(n_ctx, 2 * hd),
        ] + cast_specs,
        out_specs=[pl.BlockSpec((None, tq, KV_GROUP * hd), lambda bi, j, i: (bi, i, j))] + cast_specs,
        out_shape=[jax.ShapeDtypeStruct((b, s, h * hd), BF16)]
        + [jax.ShapeDtypeStruct(w.shape, BF16) for w in sliced],
        compiler_params=_params("parallel", "parallel", "parallel"),
        name="gqa_attention",
    )(q, k_lat, v_lat, k_ctx, v_ctx, *sliced)
    return out[0], [c.reshape(w.shape) for c, w in zip(out[1:], weights)]


def _oproj_kernel(a_ref, w_ref, x_ref, g_ref, gn_ref, sc_ref, sh_ref, x1_ref, h_ref, *, sub):
    gain_row = gn_ref[...] * (1.0 + sc_ref[...])
    for r in range(a_ref.shape[0] // sub):
        rows = slice(r * sub, (r + 1) * sub)
        y = jnp.dot(a_ref[rows, :], w_ref[...], preferred_element_type=F32)
        x1 = x_ref[rows, :] + g_ref[...] * y
        x1_ref[rows, :] = x1
        h_ref[rows, :] = (x1 * _rms_scale(x1) * gain_row + sh_ref[...]).astype(BF16)


def _out_project(a, w, x, mod, layer, gn, *, tm):
    b, s, d = x.shape
    tile = pl.BlockSpec((None, tm, d), lambda bi, i: (bi, i, 0))
    return pl.pallas_call(
        functools.partial(_oproj_kernel, sub=min(ROW_SUB, tm)),
        grid=(b, s // tm),
        in_specs=[
            tile,
            _resident((d, d), lambda bi, i: (0, 0)),
            tile,
            _mod_spec(d, layer, 2),
            pl.BlockSpec((1, d), lambda bi, i: (0, 0)),
            _mod_spec(d, layer, 4),
            _mod_spec(d, layer, 3),
        ],
        out_specs=[tile, tile],
        out_shape=[jax.ShapeDtypeStruct((b, s, d), F32), jax.ShapeDtypeStruct((b, s, d), BF16)],
        compiler_params=_params("parallel", "parallel"),
        name="out_project",
    )(a, w, x, mod, gn, mod, mod)


def _mlp_kernel(h_ref, w1_ref, w2_ref, x1_ref, g2_ref, gn_ref, sc_ref, sh_ref, *refs, final, sub):
    c = pl.program_id(2)
    last = pl.num_programs(2) - 1
    if final:
        out_ref, acc_ref = refs
    else:
        x2_ref, hn_ref, acc_ref, slab_ref = refs

    def chunk_sum(rows):
        u = jnp.maximum(jnp.dot(h_ref[rows, :], w1_ref[...], preferred_element_type=F32), 0.0)
        return jnp.dot((u * u).astype(BF16), w2_ref[...], preferred_element_type=F32)

    everything = slice(None)

    @pl.when(c == 0)
    def _():
        acc_ref[...] = chunk_sum(everything)

    @pl.when(jnp.logical_and(c > 0, c < last))
    def _():
        acc_ref[...] += chunk_sum(everything)

    @pl.when(c == last)
    def _():
        for r in range(h_ref.shape[0] // sub):
            rows = slice(r * sub, (r + 1) * sub)
            x2 = x1_ref[rows, :] + g2_ref[...] * (acc_ref[rows, :] + chunk_sum(rows))
            normed = x2 * _rms_scale(x2)
            if final:
                out_ref[rows, :] = normed * gn_ref[...]
            else:
                x2_ref[rows, :] = x2
                hn = normed * (gn_ref[...] * (1.0 + sc_ref[...])) + sh_ref[...]
                prow = slice(r * (sub // RADIX), (r + 1) * (sub // RADIX))
                for sl in range(slab_ref.shape[0]):
                    lanes = slice(sl * LANES, (sl + 1) * LANES)
                    slab_ref[sl, rows, :] = hn[:, lanes]
                    for j in range(RADIX):
                        hn_ref[j, prow, lanes] = slab_ref[
                            sl, pl.ds(r * sub + j, sub // RADIX, stride=RADIX), :].astype(BF16)


def _mlp(h, w1, w2, x1, mod, layer, gn_next, next_layer, *, tm, tf):
    b, s, d = x1.shape
    f = w1.shape[2]
    assert f // tf >= 2
    final = next_layer is None
    tile = pl.BlockSpec((None, tm, d), lambda bi, i, c: (bi, i, 0))
    row = pl.BlockSpec((1, d), lambda bi, i, c: (0, 0))
    nl = layer if final else next_layer
    out_shape = [jax.ShapeDtypeStruct((b, s, d), F32)]
    out_specs = [tile]
    scratch = [pltpu.VMEM((tm, d), F32)]
    if not final:
        out_shape.append(jax.ShapeDtypeStruct((b, RADIX, s // RADIX, d), BF16))
        out_specs.append(pl.BlockSpec((None, RADIX, tm // RADIX, d), lambda bi, i, c: (bi, 0, i, 0)))
        scratch.append(pltpu.VMEM((d // LANES, tm, LANES), F32))
    return pl.pallas_call(
        functools.partial(_mlp_kernel, final=final, sub=min(ROW_SUB, tm)),
        grid=(b, s // tm, f // tf),
        in_specs=[
            tile,
            pl.BlockSpec((None, d, tf), lambda bi, i, c: (layer, 0, c)),
            pl.BlockSpec((None, tf, d), lambda bi, i, c: (layer, c, 0)),
            tile,
            _mod_spec(d, layer, 5),
            row,
            _mod_spec(d, nl, 1),
            _mod_spec(d, nl, 0),
        ],
        out_specs=out_specs,
        out_shape=out_shape,
        scratch_shapes=scratch,
        compiler_params=_params("parallel", "parallel", "arbitrary"),
        name="mlp_final" if final else "mlp",
    )(h, w1, w2, x1, mod, gn_next, mod, mod)


def _seq_dft_mats(s, tk):
    n = s // RADIX
    k1 = np.arange(n, dtype=np.int64)[:, None]
    m = np.arange(n, dtype=np.int64)[None, :]
    mats = []
    for j in range(RADIX):
        ang = 2.0 * np.pi * (((RADIX * m + j) * k1) % s) / s
        c = (np.cos(ang) / np.sqrt(s)).reshape(n // tk, tk, n)
        sn = (-np.sin(ang) / np.sqrt(s)).reshape(n // tk, tk, n)
        mats.append(np.concatenate([c, sn], axis=1))
    return np.stack(mats).astype(np.float32)


def _chan_dft_mat():
    dg = FOURIER_GROUP_DIM
    idx = np.arange(dg, dtype=np.int64)
    ang = 2.0 * np.pi * ((idx[:, None] * idx[None, :]) % dg) / dg
    return (np.concatenate([np.cos(ang), np.sin(ang)], axis=0) / np.sqrt(dg)).astype(np.float32)


def _fourier_kernel(h0_ref, h1_ref, h2_ref, h3_ref, m_ref, cs_ref, o_ref):
    nsub, tk = m_ref.shape[1], m_ref.shape[2] // 2
    dg = FOURIER_GROUP_DIM
    cs = cs_ref[...]
    for r in range(nsub):
        a, bb = [], []
        for j, h_ref in enumerate((h0_ref, h1_ref, h2_ref, h3_ref)):
            y = jnp.dot(m_ref[j, r], h_ref[...], preferred_element_type=F32)
            a.append(y[:tk])
            bb.append(y[tk:])
        e0, e1 = a[0] + a[2], a[0] - a[2]
        f0, f1 = bb[0] + bb[2], bb[0] - bb[2]
        o0, o1 = a[1] + a[3], a[1] - a[3]
        q0, q1 = bb[1] + bb[3], bb[1] - bb[3]
        pr = (e0 + o0, e1 + q1, e0 - o0, e1 - q1)
        pi = (f0 + q0, f1 - o1, f0 - q0, f1 + o1)
        rows = slice(r * tk, (r + 1) * tk)
        for k2 in range(RADIX):
            prb, pib = pr[k2].astype(BF16), pi[k2].astype(BF16)
            for gi in range(o_ref.shape[-1] // dg):
                sl = slice(gi * dg, (gi + 1) * dg)
                y = (jnp.dot(prb[:, sl], cs[:dg], preferred_element_type=F32)
                     + jnp.dot(pib[:, sl], cs[dg:], preferred_element_type=F32))
                o_ref[k2, rows, sl] = y.astype(o_ref.dtype)


def _fourier_mix(h, *, tk, nsub, tn):
    b, _, n, d = h.shape
    s = n * RADIX
    mats = jnp.asarray(_seq_dft_mats(s, tk)).astype(BF16)
    cs = jnp.asarray(_chan_dft_mat()).astype(BF16)
    h_spec = lambda j: pl.BlockSpec((None, None, n, tn), lambda bi, c, t: (bi, j, 0, c))
    y = pl.pallas_call(
        _fourier_kernel,
        grid=(b, d // tn, n // (tk * nsub)),
        in_specs=[
            h_spec(0), h_spec(1), h_spec(2), h_spec(3),
            pl.BlockSpec((RADIX, nsub, 2 * tk, n), lambda bi, c, t: (0, t, 0, 0)),
            pl.BlockSpec((2 * FOURIER_GROUP_DIM, FOURIER_GROUP_DIM), lambda bi, c, t: (0, 0)),
        ],
        out_specs=pl.BlockSpec((None, RADIX, tk * nsub, tn), lambda bi, c, t: (bi, 0, t, c)),
        out_shape=jax.ShapeDtypeStruct((b, RADIX, n, d), BF16),
        compiler_params=_params("parallel", "parallel", "parallel"),
        name="fourier_mix",
    )(h, h, h, h, mats, cs)
    return y.reshape(b, s, d)


def _rope_tables(s):
    rows = s // GRID_W
    row = jnp.broadcast_to(jnp.arange(rows)[:, None], (rows, GRID_W)).reshape(-1)
    col = jnp.broadcast_to(jnp.arange(GRID_W)[None, :], (rows, GRID_W)).reshape(-1)
    n_freq = HEAD_DIM // 4
    inv_freq = ROPE_THETA ** (-jnp.arange(n_freq, dtype=F32) / n_freq)
    ang = jnp.concatenate([row.astype(F32)[:, None] * inv_freq, col.astype(F32)[:, None] * inv_freq], axis=-1)
    cos, sin = jnp.cos(ang), jnp.sin(ang)
    cosf = jnp.repeat(cos, 2, axis=-1)
    sinf = jnp.stack([-sin, sin], axis=-1).reshape(s, HEAD_DIM)
    return cosf, sinf


def _tiles(s, d, f):
    return dict(
        ts=min(1024, s), tq=min(512, s), tkv=min(256, s), tm_o=min(512, s), tm=min(512, s), tf=min(1024, f),
        tk=min(ROW_SUB, s // RADIX), nsub=max(1, min(2, s // RADIX // ROW_SUB)), tn=min(512, d),
    )


def kernel(x, c, ctx, c_ctx, ada_w, ada_b, norm_mix_g, norm_mlp_g, attn_w_qkv, attn_q_norm_g, attn_k_norm_g,
           attn_w_o, fourier_w_o, mlp_w1, mlp_w2, final_norm_g):
    b, s, d = x.shape
    depth = ada_w.shape[0]
    assert depth == 2 and d % (KV_GROUP * HEAD_DIM) == 0 and s % GRID_W == 0 and b <= 4
    n_heads = d // HEAD_DIM
    n_kv = n_heads // KV_GROUP
    f = mlp_w1.shape[2]
    t = _tiles(s, d, f)

    c_rows = jnp.zeros((8, d), F32).at[:b].set(c).at[4].set(c_ctx)
    mod = _modulation(c_rows, ada_w, ada_b, tn=min(1024, N_MOD * d)).reshape(depth, 8, N_MOD, 1, d)

    w_qkv = attn_w_qkv[0].astype(BF16)
    gmix = norm_mix_g.reshape(depth, 1, d)
    gmlp = norm_mlp_g.reshape(depth, 1, d)
    qg = attn_q_norm_g[0].reshape(1, HEAD_DIM)
    kg = attn_k_norm_g[0].reshape(1, HEAD_DIM)
    cosf, sinf = _rope_tables(s)

    q, k_lat, v_lat = _qkv_project(x, mod, 0, None, gmix[0], w_qkv, qg, kg, cosf, sinf,
                                   n_q_heads=n_heads, n_kv_heads=n_kv, rope=True, ts=t["ts"])
    n_ctx = ctx.shape[1]
    k_ctx, v_ctx = _qkv_project(ctx, mod, 0, 4, gmix[0], w_qkv, qg, kg, cosf, sinf,
                                n_q_heads=0, n_kv_heads=n_kv, rope=False, ts=n_ctx)
    o, (w_ao, w_fo, w1, w2) = _attention(q, k_lat, v_lat, k_ctx, v_ctx,
                                         [attn_w_o[0], fourier_w_o[0], mlp_w1, mlp_w2], tq=t["tq"], tkv=t["tkv"])
    x1, h = _out_project(o, w_ao, x, mod, 0, gmlp[0], tm=t["tm_o"])
    x2, h = _mlp(h, w1, w2, x1, mod, 0, gmix[1], 1, tm=t["tm"], tf=t["tf"])

    y = _fourier_mix(h, tk=t["tk"], nsub=t["nsub"], tn=t["tn"])
    x1, h = _out_project(y, w_fo, x2, mod, 1, gmlp[1], tm=t["tm_o"])
    (out,) = _mlp(h, w1, w2, x1, mod, 1, final_norm_g.reshape(1, d), None, tm=t["tm"], tf=t["tf"])
    return out
```

```python
import functools

import numpy as np
import jax
import jax.numpy as jnp
from jax import lax
from jax.experimental import pallas as pl
from jax.experimental.pallas import tpu as pltpu

HEAD_DIM = 128
KV_GROUP = 4
GRID_W = 64
ROPE_THETA = 10000.0
FOURIER_GROUP_DIM = 256
N_MOD = 6
EPS = 1e-6
RADIX = 8
SQRT_HALF = 0.7071067811865476
LOG2_E = 1.4426950408889634
ROW_SUB = 256
LANES = 128
BF16_SUBLANES = 16
VMEM_LIMIT_BYTES = 56 * 1024 * 1024

F32 = jnp.float32
BF16 = jnp.bfloat16


def _params(*sem):
    return pltpu.CompilerParams(dimension_semantics=sem, vmem_limit_bytes=VMEM_LIMIT_BYTES)


def _resident(shape, index_map):
    return pl.BlockSpec(shape, index_map, pipeline_mode=pl.Buffered(1))


def _rms_scale(x):
    return lax.rsqrt(jnp.mean(x * x, axis=-1, keepdims=True) + EPS)


def _mod_kernel(c_ref, w_ref, b_ref, o_ref):
    c = c_ref[...]
    s = c / (1.0 + jnp.exp(-c))
    o_ref[...] = jnp.dot(s, w_ref[...], preferred_element_type=F32) + b_ref[...]


def _modulation(c_rows, ada_w, ada_b, tn):
    depth, d, n = ada_w.shape
    return pl.pallas_call(
        _mod_kernel,
        grid=(depth, n // tn),
        in_specs=[
            pl.BlockSpec((8, d), lambda l, j: (0, 0)),
            pl.BlockSpec((None, d, tn), lambda l, j: (l, 0, j)),
            pl.BlockSpec((None, 1, tn), lambda l, j: (l, 0, j)),
        ],
        out_specs=pl.BlockSpec((None, 8, tn), lambda l, j: (l, 0, j)),
        out_shape=jax.ShapeDtypeStruct((depth, 8, n), F32),
        compiler_params=_params("parallel", "parallel"),
        name="adaln_modulation",
    )(c_rows, ada_w, ada_b.reshape(depth, 1, n))


def _mod_spec(d, layer, which, row=None):
    return pl.BlockSpec((None, None, None, 1, d),
                        lambda *g: (layer, g[0] if row is None else row, which, 0, 0))


def _qkv_kernel(x_ref, sh_ref, sc_ref, gn_ref, w_ref, qg_ref, kg_ref, cos_ref, sin_ref, *out_refs,
                n_q_heads, n_kv_heads, rope, q_scale, sub):
    if n_q_heads:
        q_ref, k_ref, v_ref = out_refs
    else:
        k_ref, v_ref = out_refs
    group_w = KV_GROUP * HEAD_DIM
    kw = n_kv_heads * HEAD_DIM
    gain_row = gn_ref[...] * (1.0 + sc_ref[...])
    for r in range(x_ref.shape[0] // sub):
        rows = slice(r * sub, (r + 1) * sub)
        x = x_ref[rows, :]
        h = (x * _rms_scale(x) * gain_row + sh_ref[...]).astype(BF16)
        if rope:
            cosf, sinf = cos_ref[rows, :], sin_ref[rows, :]
            even = (lax.broadcasted_iota(jnp.int32, cosf.shape, 1) % 2) == 0

        def head_norm_rope(t, gain):
            t = t * _rms_scale(t) * gain
            if rope:
                partner = jnp.where(even, pltpu.roll(t, HEAD_DIM - 1, 1), pltpu.roll(t, 1, 1))
                t = t * cosf + partner * sinf
            return t

        col = 0
        for g in range(n_q_heads // KV_GROUP):
            acc = jnp.dot(h, w_ref[:, col:col + group_w], preferred_element_type=F32)
            for i in range(KV_GROUP):
                t = head_norm_rope(acc[:, i * HEAD_DIM:(i + 1) * HEAD_DIM], qg_ref[...])
                q_ref[g * KV_GROUP + i, rows, :] = (t * q_scale).astype(BF16)
            col += group_w
        acc = jnp.dot(h, w_ref[:, col:col + kw], preferred_element_type=F32)
        for i in range(n_kv_heads):
            k_ref[i, rows, :] = head_norm_rope(acc[:, i * HEAD_DIM:(i + 1) * HEAD_DIM], kg_ref[...]).astype(BF16)
        col += kw
        acc = jnp.dot(h, w_ref[:, col:col + kw], preferred_element_type=F32)
        for i in range(n_kv_heads):
            v_ref[i, rows, :HEAD_DIM] = acc[:, i * HEAD_DIM:(i + 1) * HEAD_DIM].astype(BF16)
            v_ref[i, rows, HEAD_DIM:] = jnp.ones((sub, HEAD_DIM), BF16)


def _qkv_project(x, mod, layer, mod_row, gn, w, qg, kg, cosf, sinf, *, n_q_heads, n_kv_heads, rope, ts):
    b, s, d = x.shape
    ncols = (n_q_heads + 2 * n_kv_heads) * HEAD_DIM
    col_block = (w.shape[1] - ncols) // ncols
    assert col_block * ncols == w.shape[1] - ncols
    head_out = lambda nh, width=HEAD_DIM: pl.BlockSpec((None, nh, ts, width), lambda bi, i: (bi, 0, i, 0))
    head_shape = lambda nh, width=HEAD_DIM: jax.ShapeDtypeStruct((b, nh, s, width), BF16)
    out_specs = [head_out(n_kv_heads), head_out(n_kv_heads, 2 * HEAD_DIM)]
    out_shape = [head_shape(n_kv_heads), head_shape(n_kv_heads, 2 * HEAD_DIM)]
    if n_q_heads:
        out_specs.insert(0, head_out(n_q_heads))
        out_shape.insert(0, head_shape(n_q_heads))
    kernel = functools.partial(_qkv_kernel, n_q_heads=n_q_heads, n_kv_heads=n_kv_heads, rope=rope,
                               q_scale=HEAD_DIM ** -0.5 * LOG2_E, sub=min(ROW_SUB, ts))
    return pl.pallas_call(
        kernel,
        grid=(b, s // ts),
        in_specs=[
            pl.BlockSpec((None, ts, d), lambda bi, i: (bi, i, 0)),
            _mod_spec(d, layer, 0, mod_row),
            _mod_spec(d, layer, 1, mod_row),
            pl.BlockSpec((1, d), lambda bi, i: (0, 0)),
            _resident((d, ncols), lambda bi, i: (0, col_block)),
            pl.BlockSpec((1, HEAD_DIM), lambda bi, i: (0, 0)),
            pl.BlockSpec((1, HEAD_DIM), lambda bi, i: (0, 0)),
            pl.BlockSpec((ts, HEAD_DIM), lambda bi, i: (i, 0)),
            pl.BlockSpec((ts, HEAD_DIM), lambda bi, i: (i, 0)),
        ],
        out_specs=out_specs,
        out_shape=out_shape,
        compiler_params=_params("parallel", "parallel"),
        name="qkv_project" if n_q_heads else "kv_project_ctx",
    )(x, mod, mod, gn, w, qg, kg, cosf, sinf)


def _attn_kernel(q_ref, kl_ref, vl_ref, kc_ref, vc_ref, *refs, tkv, n_cast):
    cast_in, o_ref, cast_out = refs[:n_cast], refs[n_cast], refs[n_cast + 1:]
    for src, dst in zip(cast_in, cast_out):
        dst[...] = src[...].astype(dst.dtype)
    g, tq, hd = q_ref.shape
    q = q_ref[...].reshape(g * tq, hd)
    chunks = [(kl_ref, vl_ref, i * tkv, tkv) for i in range(kl_ref.shape[0] // tkv)]
    chunks.append((kc_ref, vc_ref, 0, kc_ref.shape[0]))
    m = acc = None
    for k_ref, v_ref, off, n in chunks:
        s = lax.dot_general(q, k_ref[off:off + n, :], (((1,), (1,)), ((), ())), preferred_element_type=F32)
        m_chunk = jnp.max(s, axis=-1, keepdims=True)
        m_new = m_chunk if m is None else jnp.maximum(m, m_chunk)
        pv = jnp.dot(jnp.exp2(s - m_new).astype(BF16), v_ref[off:off + n, :], preferred_element_type=F32)
        acc = pv if m is None else jnp.exp2(m - m_new) * acc + pv
        m = m_new
    o = acc[:, :hd] / acc[:, hd:]
    for i in range(g):
        o_ref[:, i * hd:(i + 1) * hd] = o[i * tq:(i + 1) * tq].astype(o_ref.dtype)


def _attention(q, k_lat, v_lat, k_ctx, v_ctx, weights, *, tq, tkv):
    b, h, s, hd = q.shape
    kvh = h // KV_GROUP
    n_ctx = k_ctx.shape[2]
    n_i = s // tq
    steps = b * kvh * n_i
    kv_spec = lambda n, width=hd: pl.BlockSpec((None, None, n, width), lambda bi, j, i: (bi, j, 0, 0))
    sliced = [w.reshape(steps, -1, w.shape[-1]) for w in weights]
    for w in sliced:
        assert w.shape[1] % BF16_SUBLANES == 0
    cast_specs = [pl.BlockSpec((None,) + w.shape[1:], lambda bi, j, i: ((bi * kvh + j) * n_i + i, 0, 0))
                  for w in sliced]
    out = pl.pallas_call(
        functools.partial(_attn_kernel, tkv=tkv, n_cast=len(sliced)),
        grid=(b, kvh, n_i),
        in_specs=[
            pl.BlockSpec((None, KV_GROUP, tq, hd), lambda bi, j, i: (bi, j, i, 0)),
            kv_spec(s), kv_spec(s, 2 * hd), kv_spec(n_ctx), kv_spec(n_ctx, 2 * hd),
        ] + cast_specs,
        out_specs=[pl.BlockSpec((None, tq, KV_GROUP * hd), lambda bi, j, i: (bi, i, j))] + cast_specs,
        out_shape=[jax.ShapeDtypeStruct((b, s, h * hd), BF16)]
        + [jax.ShapeDtypeStruct(w.shape, BF16) for w in sliced],
        compiler_params=_params("parallel", "parallel", "parallel"),
        name="gqa_attention",
    )(q, k_lat, v_lat, k_ctx, v_ctx, *sliced)
    return out[0], [c.reshape(w.shape) for c, w in zip(out[1:], weights)]


def _oproj_kernel(a_ref, w_ref, x_ref, g_ref, gn_ref, sc_ref, sh_ref, x1_ref, h_ref, *, sub):
    gain_row = gn_ref[...] * (1.0 + sc_ref[...])
    for r in range(a_ref.shape[0] // sub):
        rows = slice(r * sub, (r + 1) * sub)
        y = jnp.dot(a_ref[rows, :], w_ref[...], preferred_element_type=F32)
        x1 = x_ref[rows, :] + g_ref[...] * y
        x1_ref[rows, :] = x1
        h_ref[rows, :] = (x1 * _rms_scale(x1) * gain_row + sh_ref[...]).astype(BF16)


def _out_project(a, w, x, mod, layer, gn, *, tm):
    b, s, d = x.shape
    tile = pl.BlockSpec((None, tm, d), lambda bi, i: (bi, i, 0))
    return pl.pallas_call(
        functools.partial(_oproj_kernel, sub=min(ROW_SUB, tm)),
        grid=(b, s // tm),
        in_specs=[
            tile,
            _resident((d, d), lambda bi, i: (0, 0)),
            tile,
            _mod_spec(d, layer, 2),
            pl.BlockSpec((1, d), lambda bi, i: (0, 0)),
            _mod_spec(d, layer, 4),
            _mod_spec(d, layer, 3),
        ],
        out_specs=[tile, tile],
        out_shape=[jax.ShapeDtypeStruct((b, s, d), F32), jax.ShapeDtypeStruct((b, s, d), BF16)],
        compiler_params=_params("parallel", "parallel"),
        name="out_project",
    )(a, w, x, mod, gn, mod, mod)


def _mlp_kernel(h_ref, w1_ref, w2_ref, x1_ref, g2_ref, gn_ref, sc_ref, sh_ref, *refs, final, sub):
    c = pl.program_id(2)
    last = pl.num_programs(2) - 1
    if final:
        out_ref, acc_ref = refs
    else:
        x2_ref, hn_ref, acc_ref, slab_ref = refs

    def chunk_sum(rows):
        u = jnp.maximum(jnp.dot(h_ref[rows, :], w1_ref[...], preferred_element_type=F32), 0.0)
        return jnp.dot((u * u).astype(BF16), w2_ref[...], preferred_element_type=F32)

    everything = slice(None)

    @pl.when(c == 0)
    def _():
        acc_ref[...] = chunk_sum(everything)

    @pl.when(jnp.logical_and(c > 0, c < last))
    def _():
        acc_ref[...] += chunk_sum(everything)

    @pl.when(c == last)
    def _():
        for r in range(h_ref.shape[0] // sub):
            rows = slice(r * sub, (r + 1) * sub)
            x2 = x1_ref[rows, :] + g2_ref[...] * (acc_ref[rows, :] + chunk_sum(rows))
            normed = x2 * _rms_scale(x2)
            if final:
                out_ref[rows, :] = normed * gn_ref[...]
            else:
                x2_ref[rows, :] = x2
                hn = normed * (gn_ref[...] * (1.0 + sc_ref[...])) + sh_ref[...]
                prow = slice(r * (sub // RADIX), (r + 1) * (sub // RADIX))
                for sl in range(slab_ref.shape[0]):
                    lanes = slice(sl * LANES, (sl + 1) * LANES)
                    slab_ref[sl, rows, :] = hn[:, lanes]
                    for j in range(RADIX):
                        hn_ref[j, prow, lanes] = slab_ref[
                            sl, pl.ds(r * sub + j, sub // RADIX, stride=RADIX), :].astype(BF16)


def _mlp(h, w1, w2, x1, mod, layer, gn_next, next_layer, *, tm, tf):
    b, s, d = x1.shape
    f = w1.shape[2]
    assert f // tf >= 2
    final = next_layer is None
    tile = pl.BlockSpec((None, tm, d), lambda bi, i, c: (bi, i, 0))
    row = pl.BlockSpec((1, d), lambda bi, i, c: (0, 0))
    nl = layer if final else next_layer
    out_shape = [jax.ShapeDtypeStruct((b, s, d), F32)]
    out_specs = [tile]
    scratch = [pltpu.VMEM((tm, d), F32)]
    if not final:
        out_shape.append(jax.ShapeDtypeStruct((b, RADIX, s // RADIX, d), BF16))
        out_specs.append(pl.BlockSpec((None, RADIX, tm // RADIX, d), lambda bi, i, c: (bi, 0, i, 0)))
        scratch.append(pltpu.VMEM((d // LANES, tm, LANES), F32))
    return pl.pallas_call(
        functools.partial(_mlp_kernel, final=final, sub=min(ROW_SUB, tm)),
        grid=(b, s // tm, f // tf),
        in_specs=[
            tile,
            pl.BlockSpec((None, d, tf), lambda bi, i, c: (layer, 0, c)),
            pl.BlockSpec((None, tf, d), lambda bi, i, c: (layer, c, 0)),
            tile,
            _mod_spec(d, layer, 5),
            row,
            _mod_spec(d, nl, 1),
            _mod_spec(d, nl, 0),
        ],
        out_specs=out_specs,
        out_shape=out_shape,
        scratch_shapes=scratch,
        compiler_params=_params("parallel", "parallel", "arbitrary"),
        name="mlp_final" if final else "mlp",
    )(h, w1, w2, x1, mod, gn_next, mod, mod)


def _seq_dft_mats(s, tk):
    n = s // RADIX
    k1 = np.arange(n, dtype=np.int64)[:, None]
    m = np.arange(n, dtype=np.int64)[None, :]
    mats = []
    for j in range(RADIX):
        ang = 2.0 * np.pi * (((RADIX * m + j) * k1) % s) / s
        c = (np.cos(ang) / np.sqrt(s)).reshape(n // tk, tk, n)
        sn = (-np.sin(ang) / np.sqrt(s)).reshape(n // tk, tk, n)
        mats.append(np.concatenate([c, sn], axis=1))
    return np.stack(mats).astype(np.float32)


def _chan_dft_mat():
    dg = FOURIER_GROUP_DIM
    idx = np.arange(dg, dtype=np.int64)
    ang = 2.0 * np.pi * ((idx[:, None] * idx[None, :]) % dg) / dg
    return (np.concatenate([np.cos(ang), np.sin(ang)], axis=0) / np.sqrt(dg)).astype(np.float32)


def _dft4(z):
    (ar, ai), (br, bi), (cr, ci), (dr, di) = z
    er, ei, fr, fi = ar + cr, ai + ci, ar - cr, ai - ci
    gr, gi, hr, hi = br + dr, bi + di, br - dr, bi - di
    return [(er + gr, ei + gi), (fr + hi, fi - hr), (er - gr, ei - gi), (fr - hi, fi + hr)]


def _fourier_kernel(*refs):
    h_refs, (m_ref, cs_ref, o_ref) = refs[:RADIX], refs[RADIX:]
    nsub, tk = m_ref.shape[1], m_ref.shape[2] // 2
    dg = FOURIER_GROUP_DIM
    cs = cs_ref[...]
    for r in range(nsub):
        y = []
        for j, h_ref in enumerate(h_refs):
            yj = jnp.dot(m_ref[j, r], h_ref[...], preferred_element_type=F32)
            y.append((yj[:tk], yj[tk:]))
        half = RADIX // 2
        u = [(y[j][0] + y[j + half][0], y[j][1] + y[j + half][1]) for j in range(half)]
        v = [(y[j][0] - y[j + half][0], y[j][1] - y[j + half][1]) for j in range(half)]
        w = [v[0],
             ((v[1][0] + v[1][1]) * SQRT_HALF, (v[1][1] - v[1][0]) * SQRT_HALF),
             (v[2][1], -v[2][0]),
             ((v[3][1] - v[3][0]) * SQRT_HALF, (-v[3][0] - v[3][1]) * SQRT_HALF)]
        even, odd = _dft4(u), _dft4(w)
        rows = slice(r * tk, (r + 1) * tk)
        for k2 in range(RADIX):
            pr, pi = (even if k2 % 2 == 0 else odd)[k2 // 2]
            prb, pib = pr.astype(BF16), pi.astype(BF16)
            for gi in range(o_ref.shape[-1] // dg):
                sl = slice(gi * dg, (gi + 1) * dg)
                out = (jnp.dot(prb[:, sl], cs[:dg], preferred_element_type=F32)
                       + jnp.dot(pib[:, sl], cs[dg:], preferred_element_type=F32))
                o_ref[k2, rows, sl] = out.astype(o_ref.dtype)


def _fourier_mix(h, *, tk, nsub, tn):
    b, _, n, d = h.shape
    s = n * RADIX
    mats = jnp.asarray(_seq_dft_mats(s, tk)).astype(BF16)
    cs = jnp.asarray(_chan_dft_mat()).astype(BF16)
    h_specs = [pl.BlockSpec((None, None, n, tn), lambda bi, c, t, j=j: (bi, j, 0, c)) for j in range(RADIX)]
    y = pl.pallas_call(
        _fourier_kernel,
        grid=(b, d // tn, n // (tk * nsub)),
        in_specs=h_specs + [
            pl.BlockSpec((RADIX, nsub, 2 * tk, n), lambda bi, c, t: (0, t, 0, 0)),
            pl.BlockSpec((2 * FOURIER_GROUP_DIM, FOURIER_GROUP_DIM), lambda bi, c, t: (0, 0)),
        ],
        out_specs=pl.BlockSpec((None, RADIX, tk * nsub, tn), lambda bi, c, t: (bi, 0, t, c)),
        out_shape=jax.ShapeDtypeStruct((b, RADIX, n, d), BF16),
        compiler_params=_params("parallel", "parallel", "parallel"),
        name="fourier_mix",
    )(*([h] * RADIX), mats, cs)
    return y.reshape(b, s, d)


def _rope_tables(s):
    rows = s // GRID_W
    row = jnp.broadcast_to(jnp.arange(rows)[:, None], (rows, GRID_W)).reshape(-1)
    col = jnp.broadcast_to(jnp.arange(GRID_W)[None, :], (rows, GRID_W)).reshape(-1)
    n_freq = HEAD_DIM // 4
    inv_freq = ROPE_THETA ** (-jnp.arange(n_freq, dtype=F32) / n_freq)
    ang = jnp.concatenate([row.astype(F32)[:, None] * inv_freq, col.astype(F32)[:, None] * inv_freq], axis=-1)
    cos, sin = jnp.cos(ang), jnp.sin(ang)
    cosf = jnp.repeat(cos, 2, axis=-1)
    sinf = jnp.stack([-sin, sin], axis=-1).reshape(s, HEAD_DIM)
    return cosf, sinf


def _tiles(s, d, f):
    return dict(
        ts=min(1024, s), tq=min(512, s), tkv=min(256, s), tm_o=min(512, s), tm=min(512, s), tf=min(1024, f),
        tk=min(ROW_SUB, s // RADIX), nsub=max(1, min(2, s // RADIX // ROW_SUB)), tn=min(512, d),
    )


def kernel(x, c, ctx, c_ctx, ada_w, ada_b, norm_mix_g, norm_mlp_g, attn_w_qkv, attn_q_norm_g, attn_k_norm_g,
           attn_w_o, fourier_w_o, mlp_w1, mlp_w2, final_norm_g):
    b, s, d = x.shape
    depth = ada_w.shape[0]
    assert depth == 2 and d % (KV_GROUP * HEAD_DIM) == 0 and s % GRID_W == 0 and b <= 4
    n_heads = d // HEAD_DIM
    n_kv = n_heads // KV_GROUP
    f = mlp_w1.shape[2]
    t = _tiles(s, d, f)

    c_rows = jnp.zeros((8, d), F32).at[:b].set(c).at[4].set(c_ctx)
    mod = _modulation(c_rows, ada_w, ada_b, tn=min(1024, N_MOD * d)).reshape(depth, 8, N_MOD, 1, d)

    w_qkv = attn_w_qkv[0].astype(BF16)
    gmix = norm_mix_g.reshape(depth, 1, d)
    gmlp = norm_mlp_g.reshape(depth, 1, d)
    qg = attn_q_norm_g[0].reshape(1, HEAD_DIM)
    kg = attn_k_norm_g[0].reshape(1, HEAD_DIM)
    cosf, sinf = _rope_tables(s)

    q, k_lat, v_lat = _qkv_project(x, mod, 0, None, gmix[0], w_qkv, qg, kg, cosf, sinf,
                                   n_q_heads=n_heads, n_kv_heads=n_kv, rope=True, ts=t["ts"])
    n_ctx = ctx.shape[1]
    k_ctx, v_ctx = _qkv_project(ctx, mod, 0, 4, gmix[0], w_qkv, qg, kg, cosf, sinf,
                                n_q_heads=0, n_kv_heads=n_kv, rope=False, ts=n_ctx)
    o, (w_ao, w_fo, w1, w2) = _attention(q, k_lat, v_lat, k_ctx, v_ctx,
                                         [attn_w_o[0], fourier_w_o[0], mlp_w1, mlp_w2], tq=t["tq"], tkv=t["tkv"])
    x1, h = _out_project(o, w_ao, x, mod, 0, gmlp[0], tm=t["tm_o"])
    x2, h = _mlp(h, w1, w2, x1, mod, 0, gmix[1], 1, tm=t["tm"], tf=t["tf"])

    y = _fourier_mix(h, tk=t["tk"], nsub=t["nsub"], tn=t["tn"])
    x1, h = _out_project(y, w_fo, x2, mod, 1, gmlp[1], tm=t["tm_o"])
    (out,) = _mlp(h, w1, w2, x1, mod, 1, final_norm_g.reshape(1, d), None, tm=t["tm"], tf=t["tf"])
    return out
```

```python
import functools

import numpy as np
import jax
import jax.numpy as jnp
from jax import lax
from jax.experimental import pallas as pl
from jax.experimental.pallas import tpu as pltpu

HEAD_DIM = 128
KV_GROUP = 4
GRID_W = 64
ROPE_THETA = 10000.0
FOURIER_GROUP_DIM = 256
N_MOD = 6
EPS = 1e-6
RADIX = 8
SQRT_HALF = 0.7071067811865476
LOG2_E = 1.4426950408889634
ROW_SUB = 256
LANES = 128
BF16_SUBLANES = 16
VMEM_LIMIT_BYTES = 60 * 1024 * 1024

F32 = jnp.float32
BF16 = jnp.bfloat16


def _params(*sem):
    return pltpu.CompilerParams(dimension_semantics=sem, vmem_limit_bytes=VMEM_LIMIT_BYTES)


def _resident(shape, index_map):
    return pl.BlockSpec(shape, index_map, pipeline_mode=pl.Buffered(1))


def _rms_scale(x):
    return lax.rsqrt(jnp.mean(x * x, axis=-1, keepdims=True) + EPS)


def _mod_kernel(c_ref, w_ref, b_ref, o_ref):
    c = c_ref[...]
    s = c / (1.0 + jnp.exp(-c))
    o_ref[...] = jnp.dot(s, w_ref[...], preferred_element_type=F32) + b_ref[...]


def _modulation(c_rows, ada_w, ada_b, tn):
    depth, d, n = ada_w.shape
    return pl.pallas_call(
        _mod_kernel,
        grid=(depth, n // tn),
        in_specs=[
            pl.BlockSpec((8, d), lambda l, j: (0, 0)),
            pl.BlockSpec((None, d, tn), lambda l, j: (l, 0, j)),
            pl.BlockSpec((None, 1, tn), lambda l, j: (l, 0, j)),
        ],
        out_specs=pl.BlockSpec((None, 8, tn), lambda l, j: (l, 0, j)),
        out_shape=jax.ShapeDtypeStruct((depth, 8, n), F32),
        compiler_params=_params("parallel", "parallel"),
        name="adaln_modulation",
    )(c_rows, ada_w, ada_b.reshape(depth, 1, n))


def _mod_spec(d, layer, which, row=None):
    return pl.BlockSpec((None, None, None, 1, d),
                        lambda *g: (layer, g[0] if row is None else row, which, 0, 0))


def _qkv_kernel(x_ref, sh_ref, sc_ref, gn_ref, w_ref, qg_ref, kg_ref, cos_ref, sin_ref, *out_refs,
                n_q_heads, n_kv_heads, rope, q_scale, sub):
    if n_q_heads:
        q_ref, k_ref, v_ref = out_refs
    else:
        k_ref, v_ref = out_refs
    group_w = KV_GROUP * HEAD_DIM
    kw = n_kv_heads * HEAD_DIM
    gain_row = gn_ref[...] * (1.0 + sc_ref[...])
    for r in range(x_ref.shape[0] // sub):
        rows = slice(r * sub, (r + 1) * sub)
        x = x_ref[rows, :]
        h = (x * _rms_scale(x) * gain_row + sh_ref[...]).astype(BF16)
        if rope:
            cosf, sinf = cos_ref[rows, :], sin_ref[rows, :]
            even = (lax.broadcasted_iota(jnp.int32, cosf.shape, 1) % 2) == 0

        def head_norm_rope(t, gain):
            t = t * _rms_scale(t) * gain
            if rope:
                partner = jnp.where(even, pltpu.roll(t, HEAD_DIM - 1, 1), pltpu.roll(t, 1, 1))
                t = t * cosf + partner * sinf
            return t

        col = 0
        for g in range(n_q_heads // KV_GROUP):
            acc = jnp.dot(h, w_ref[:, col:col + group_w], preferred_element_type=F32)
            for i in range(KV_GROUP):
                t = head_norm_rope(acc[:, i * HEAD_DIM:(i + 1) * HEAD_DIM], qg_ref[...])
                q_ref[g * KV_GROUP + i, rows, :] = (t * q_scale).astype(BF16)
            col += group_w
        acc = jnp.dot(h, w_ref[:, col:col + kw], preferred_element_type=F32)
        for i in range(n_kv_heads):
            k_ref[i, rows, :] = head_norm_rope(acc[:, i * HEAD_DIM:(i + 1) * HEAD_DIM], kg_ref[...]).astype(BF16)
        col += kw
        acc = jnp.dot(h, w_ref[:, col:col + kw], preferred_element_type=F32)
        for i in range(n_kv_heads):
            v_ref[i, rows, :HEAD_DIM] = acc[:, i * HEAD_DIM:(i + 1) * HEAD_DIM].astype(BF16)
            v_ref[i, rows, HEAD_DIM:] = jnp.ones((sub, HEAD_DIM), BF16)


def _qkv_project(x, mod, layer, mod_row, gn, w, qg, kg, cosf, sinf, *, n_q_heads, n_kv_heads, rope, ts):
    b, s, d = x.shape
    ncols = (n_q_heads + 2 * n_kv_heads) * HEAD_DIM
    col_block = (w.shape[1] - ncols) // ncols
    assert col_block * ncols == w.shape[1] - ncols
    head_out = lambda nh, width=HEAD_DIM: pl.BlockSpec((None, nh, ts, width), lambda bi, i: (bi, 0, i, 0))
    head_shape = lambda nh, width=HEAD_DIM: jax.ShapeDtypeStruct((b, nh, s, width), BF16)
    out_specs = [head_out(n_kv_heads), head_out(n_kv_heads, 2 * HEAD_DIM)]
    out_shape = [head_shape(n_kv_heads), head_shape(n_kv_heads, 2 * HEAD_DIM)]
    if n_q_heads:
        out_specs.insert(0, head_out(n_q_heads))
        out_shape.insert(0, head_shape(n_q_heads))
    kernel = functools.partial(_qkv_kernel, n_q_heads=n_q_heads, n_kv_heads=n_kv_heads, rope=rope,
                               q_scale=HEAD_DIM ** -0.5 * LOG2_E, sub=min(ROW_SUB, ts))
    return pl.pallas_call(
        kernel,
        grid=(b, s // ts),
        in_specs=[
            pl.BlockSpec((None, ts, d), lambda bi, i: (bi, i, 0)),
            _mod_spec(d, layer, 0, mod_row),
            _mod_spec(d, layer, 1, mod_row),
            pl.BlockSpec((1, d), lambda bi, i: (0, 0)),
            _resident((d, ncols), lambda bi, i: (0, col_block)),
            pl.BlockSpec((1, HEAD_DIM), lambda bi, i: (0, 0)),
            pl.BlockSpec((1, HEAD_DIM), lambda bi, i: (0, 0)),
            pl.BlockSpec((ts, HEAD_DIM), lambda bi, i: (i, 0)),
            pl.BlockSpec((ts, HEAD_DIM), lambda bi, i: (i, 0)),
        ],
        out_specs=out_specs,
        out_shape=out_shape,
        compiler_params=_params("parallel", "parallel"),
        name="qkv_project" if n_q_heads else "kv_project_ctx",
    )(x, mod, mod, gn, w, qg, kg, cosf, sinf)


def _attn_kernel(q_ref, kl_ref, vl_ref, kc_ref, vc_ref, *refs, tkv, n_cast):
    cast_in, o_ref, cast_out = refs[:n_cast], refs[n_cast], refs[n_cast + 1:]
    for src, dst in zip(cast_in, cast_out):
        dst[...] = src[...].astype(dst.dtype)
    g, tq, hd = q_ref.shape
    q = q_ref[...].reshape(g * tq, hd)
    chunks = [(kl_ref, vl_ref, i * tkv, tkv) for i in range(kl_ref.shape[0] // tkv)]
    chunks.append((kc_ref, vc_ref, 0, kc_ref.shape[0]))
    m = acc = None
    for k_ref, v_ref, off, n in chunks:
        s = lax.dot_general(q, k_ref[off:off + n, :], (((1,), (1,)), ((), ())), preferred_element_type=F32)
        m_chunk = jnp.max(s, axis=-1, keepdims=True)
        m_new = m_chunk if m is None else jnp.maximum(m, m_chunk)
        pv = jnp.dot(jnp.exp2(s - m_new).astype(BF16), v_ref[off:off + n, :], preferred_element_type=F32)
        acc = pv if m is None else jnp.exp2(m - m_new) * acc + pv
        m = m_new
    o = acc[:, :hd] / acc[:, hd:]
    for i in range(g):
        o_ref[:, i * hd:(i + 1) * hd] = o[i * tq:(i + 1) * tq].astype(o_ref.dtype)


def _attention(q, k_lat, v_lat, k_ctx, v_ctx, weights, *, tq, tkv):
    b, h, s, hd = q.shape
    kvh = h // KV_GROUP
    n_ctx = k_ctx.shape[2]
    n_i = s // tq
    steps = b * kvh * n_i
    kv_spec = lambda n, width=hd: pl.BlockSpec((None, None, n, width), lambda bi, j, i: (bi, j, 0, 0))
    sliced = [w.reshape(steps, -1, w.shape[-1]) for w in weights]
    for w in sliced:
        assert w.shape[1] % BF16_SUBLANES == 0
    cast_specs = [pl.BlockSpec((None,) + w.shape[1:], lambda bi, j, i: ((bi * kvh + j) * n_i + i, 0, 0))
                  for w in sliced]
    out = pl.pallas_call(
        functools.partial(_attn_kernel, tkv=tkv, n_cast=len(sliced)),
        grid=(b, kvh, n_i),
        in_specs=[
            pl.BlockSpec((None, KV_GROUP, tq, hd), lambda bi, j, i: (bi, j, i, 0)),
            kv_spec(s), kv_spec(s, 2 * hd), kv_spec(n_ctx), kv_spec(n_ctx, 2 * hd),
        ] + cast_specs,
        out_specs=[pl.BlockSpec((None, tq, KV_GROUP * hd), lambda bi, j, i: (bi, i, j))] + cast_specs,
        out_shape=[jax.ShapeDtypeStruct((b, s, h * hd), BF16)]
        + [jax.ShapeDtypeStruct(w.shape, BF16) for w in sliced],
        compiler_params=_params("parallel", "parallel", "parallel"),
        name="gqa_attention",
    )(q, k_lat, v_lat, k_ctx, v_ctx, *sliced)
    return out[0], [c.reshape(w.shape) for c, w in zip(out[1:], weights)]


def _oproj_kernel(a_ref, w_ref, x_ref, g_ref, gn_ref, sc_ref, sh_ref, x1_ref, h_ref, *, sub):
    gain_row = gn_ref[...] * (1.0 + sc_ref[...])
    for r in range(a_ref.shape[0] // sub):
        rows = slice(r * sub, (r + 1) * sub)
        y = jnp.dot(a_ref[rows, :], w_ref[...], preferred_element_type=F32)
        x1 = x_ref[rows, :] + g_ref[...] * y
        x1_ref[rows, :] = x1
        h_ref[rows, :] = (x1 * _rms_scale(x1) * gain_row + sh_ref[...]).astype(BF16)


def _out_project(a, w, x, mod, layer, gn, *, tm):
    b, s, d = x.shape
    tile = pl.BlockSpec((None, tm, d), lambda bi, i: (bi, i, 0))
    return pl.pallas_call(
        functools.partial(_oproj_kernel, sub=min(ROW_SUB, tm)),
        grid=(b, s // tm),
        in_specs=[
            tile,
            _resident((d, d), lambda bi, i: (0, 0)),
            tile,
            _mod_spec(d, layer, 2),
            pl.BlockSpec((1, d), lambda bi, i: (0, 0)),
            _mod_spec(d, layer, 4),
            _mod_spec(d, layer, 3),
        ],
        out_specs=[tile, tile],
        out_shape=[jax.ShapeDtypeStruct((b, s, d), F32), jax.ShapeDtypeStruct((b, s, d), BF16)],
        compiler_params=_params("parallel", "parallel"),
        name="out_project",
    )(a, w, x, mod, gn, mod, mod)


def _mlp_kernel(h_ref, w1_ref, w2_ref, x1_ref, g2_ref, gn_ref, sc_ref, sh_ref, *refs, final, sub):
    c = pl.program_id(2)
    last = pl.num_programs(2) - 1
    if final:
        out_ref, acc_ref = refs
    else:
        x2_ref, hn_ref, acc_ref, slab_ref = refs

    def chunk_sum(rows):
        u = jnp.maximum(jnp.dot(h_ref[rows, :], w1_ref[...], preferred_element_type=F32), 0.0)
        return jnp.dot((u * u).astype(BF16), w2_ref[...], preferred_element_type=F32)

    everything = slice(None)

    @pl.when(c == 0)
    def _():
        acc_ref[...] = chunk_sum(everything)

    @pl.when(jnp.logical_and(c > 0, c < last))
    def _():
        acc_ref[...] += chunk_sum(everything)

    @pl.when(c == last)
    def _():
        for r in range(h_ref.shape[0] // sub):
            rows = slice(r * sub, (r + 1) * sub)
            x2 = x1_ref[rows, :] + g2_ref[...] * (acc_ref[rows, :] + chunk_sum(rows))
            normed = x2 * _rms_scale(x2)
            if final:
                out_ref[rows, :] = normed * gn_ref[...]
            else:
                x2_ref[rows, :] = x2
                hn = normed * (gn_ref[...] * (1.0 + sc_ref[...])) + sh_ref[...]
                prow = slice(r * (sub // RADIX), (r + 1) * (sub // RADIX))
                for sl in range(slab_ref.shape[0]):
                    lanes = slice(sl * LANES, (sl + 1) * LANES)
                    slab_ref[sl, rows, :] = hn[:, lanes]
                    for j in range(RADIX):
                        hn_ref[j, prow, lanes] = slab_ref[
                            sl, pl.ds(r * sub + j, sub // RADIX, stride=RADIX), :].astype(BF16)


def _mlp(h, w1, w2, x1, mod, layer, gn_next, next_layer, *, tm, tf):
    b, s, d = x1.shape
    f = w1.shape[2]
    assert f // tf >= 2
    final = next_layer is None
    tile = pl.BlockSpec((None, tm, d), lambda bi, i, c: (bi, i, 0))
    row = pl.BlockSpec((1, d), lambda bi, i, c: (0, 0))
    nl = layer if final else next_layer
    out_shape = [jax.ShapeDtypeStruct((b, s, d), F32)]
    out_specs = [tile]
    scratch = [pltpu.VMEM((tm, d), F32)]
    if not final:
        out_shape.append(jax.ShapeDtypeStruct((b, RADIX, s // RADIX, d), BF16))
        out_specs.append(pl.BlockSpec((None, RADIX, tm // RADIX, d), lambda bi, i, c: (bi, 0, i, 0)))
        scratch.append(pltpu.VMEM((d // LANES, tm, LANES), F32))
    return pl.pallas_call(
        functools.partial(_mlp_kernel, final=final, sub=min(ROW_SUB, tm)),
        grid=(b, s // tm, f // tf),
        in_specs=[
            tile,
            pl.BlockSpec((None, d, tf), lambda bi, i, c: (layer, 0, c)),
            pl.BlockSpec((None, tf, d), lambda bi, i, c: (layer, c, 0)),
            tile,
            _mod_spec(d, layer, 5),
            row,
            _mod_spec(d, nl, 1),
            _mod_spec(d, nl, 0),
        ],
        out_specs=out_specs,
        out_shape=out_shape,
        scratch_shapes=scratch,
        compiler_params=_params("parallel", "parallel", "arbitrary"),
        name="mlp_final" if final else "mlp",
    )(h, w1, w2, x1, mod, gn_next, mod, mod)


def _seq_dft_mats(s, tk):
    n = s // RADIX
    k1 = np.arange(n, dtype=np.int64)[:, None]
    m = np.arange(n, dtype=np.int64)[None, :]
    mats = []
    for j in range(RADIX):
        ang = 2.0 * np.pi * (((RADIX * m + j) * k1) % s) / s
        c = (np.cos(ang) / np.sqrt(s)).reshape(n // tk, tk, n)
        sn = (-np.sin(ang) / np.sqrt(s)).reshape(n // tk, tk, n)
        mats.append(np.concatenate([c, sn], axis=1))
    return np.stack(mats).astype(np.float32)


def _chan_dft_mat():
    dg = FOURIER_GROUP_DIM
    idx = np.arange(dg, dtype=np.int64)
    ang = 2.0 * np.pi * ((idx[:, None] * idx[None, :]) % dg) / dg
    return (np.concatenate([np.cos(ang), np.sin(ang)], axis=0) / np.sqrt(dg)).astype(np.float32)


def _dft4(z):
    (ar, ai), (br, bi), (cr, ci), (dr, di) = z
    er, ei, fr, fi = ar + cr, ai + ci, ar - cr, ai - ci
    gr, gi, hr, hi = br + dr, bi + di, br - dr, bi - di
    return [(er + gr, ei + gi), (fr + hi, fi - hr), (er - gr, ei - gi), (fr - hi, fi + hr)]


def _fourier_kernel(*refs):
    h_refs, (m_ref, cs_ref, o_ref) = refs[:RADIX], refs[RADIX:]
    nsub, tk = m_ref.shape[1], m_ref.shape[2] // 2
    dg = FOURIER_GROUP_DIM
    cs = cs_ref[...]
    for r in range(nsub):
        y = []
        for j, h_ref in enumerate(h_refs):
            yj = jnp.dot(m_ref[j, r], h_ref[...], preferred_element_type=F32)
            y.append((yj[:tk], yj[tk:]))
        half = RADIX // 2
        u = [(y[j][0] + y[j + half][0], y[j][1] + y[j + half][1]) for j in range(half)]
        v = [(y[j][0] - y[j + half][0], y[j][1] - y[j + half][1]) for j in range(half)]
        w = [v[0],
             ((v[1][0] + v[1][1]) * SQRT_HALF, (v[1][1] - v[1][0]) * SQRT_HALF),
             (v[2][1], -v[2][0]),
             ((v[3][1] - v[3][0]) * SQRT_HALF, (-v[3][0] - v[3][1]) * SQRT_HALF)]
        even, odd = _dft4(u), _dft4(w)
        rows = slice(r * tk, (r + 1) * tk)
        for k2 in range(RADIX):
            pr, pi = (even if k2 % 2 == 0 else odd)[k2 // 2]
            prb, pib = pr.astype(BF16), pi.astype(BF16)
            for gi in range(o_ref.shape[-1] // dg):
                sl = slice(gi * dg, (gi + 1) * dg)
                out = (jnp.dot(prb[:, sl], cs[:dg], preferred_element_type=F32)
                       + jnp.dot(pib[:, sl], cs[dg:], preferred_element_type=F32))
                o_ref[k2, rows, sl] = out.astype(o_ref.dtype)


def _fourier_mix(h, *, tk, nsub, tn):
    b, _, n, d = h.shape
    s = n * RADIX
    mats = jnp.asarray(_seq_dft_mats(s, tk)).astype(BF16)
    cs = jnp.asarray(_chan_dft_mat()).astype(BF16)
    h_specs = [pl.BlockSpec((None, None, n, tn), lambda bi, c, t, j=j: (bi, j, 0, c)) for j in range(RADIX)]
    y = pl.pallas_call(
        _fourier_kernel,
        grid=(b, d // tn, n // (tk * nsub)),
        in_specs=h_specs + [
            pl.BlockSpec((RADIX, nsub, 2 * tk, n), lambda bi, c, t: (0, t, 0, 0)),
            pl.BlockSpec((2 * FOURIER_GROUP_DIM, FOURIER_GROUP_DIM), lambda bi, c, t: (0, 0)),
        ],
        out_specs=pl.BlockSpec((None, RADIX, tk * nsub, tn), lambda bi, c, t: (bi, 0, t, c)),
        out_shape=jax.ShapeDtypeStruct((b, RADIX, n, d), BF16),
        compiler_params=_params("parallel", "parallel", "parallel"),
        name="fourier_mix",
    )(*([h] * RADIX), mats, cs)
    return y.reshape(b, s, d)


def _rope_tables(s):
    rows = s // GRID_W
    row = jnp.broadcast_to(jnp.arange(rows)[:, None], (rows, GRID_W)).reshape(-1)
    col = jnp.broadcast_to(jnp.arange(GRID_W)[None, :], (rows, GRID_W)).reshape(-1)
    n_freq = HEAD_DIM // 4
    inv_freq = ROPE_THETA ** (-jnp.arange(n_freq, dtype=F32) / n_freq)
    ang = jnp.concatenate([row.astype(F32)[:, None] * inv_freq, col.astype(F32)[:, None] * inv_freq], axis=-1)
    cos, sin = jnp.cos(ang), jnp.sin(ang)
    cosf = jnp.repeat(cos, 2, axis=-1)
    sinf = jnp.stack([-sin, sin], axis=-1).reshape(s, HEAD_DIM)
    return cosf, sinf


def _tiles(s, d, f):
    return dict(
        ts=min(1024, s), tq=min(1024, s), tkv=min(256, s), tm_o=min(1024, s), tm=min(512, s), tf=min(1024, f),
        tk=min(ROW_SUB, s // RADIX), nsub=max(1, min(2, s // RADIX // ROW_SUB)), tn=min(512, d),
    )


def kernel(x, c, ctx, c_ctx, ada_w, ada_b, norm_mix_g, norm_mlp_g, attn_w_qkv, attn_q_norm_g, attn_k_norm_g,
           attn_w_o, fourier_w_o, mlp_w1, mlp_w2, final_norm_g):
    b, s, d = x.shape
    depth = ada_w.shape[0]
    assert depth == 2 and d % (KV_GROUP * HEAD_DIM) == 0 and s % GRID_W == 0 and b <= 4
    n_heads = d // HEAD_DIM
    n_kv = n_heads // KV_GROUP
    f = mlp_w1.shape[2]
    t = _tiles(s, d, f)

    c_rows = jnp.zeros((8, d), F32).at[:b].set(c).at[4].set(c_ctx)
    mod = _modulation(c_rows, ada_w, ada_b, tn=min(1024, N_MOD * d)).reshape(depth, 8, N_MOD, 1, d)

    w_qkv = attn_w_qkv[0].astype(BF16)
    gmix = norm_mix_g.reshape(depth, 1, d)
    gmlp = norm_mlp_g.reshape(depth, 1, d)
    qg = attn_q_norm_g[0].reshape(1, HEAD_DIM)
    kg = attn_k_norm_g[0].reshape(1, HEAD_DIM)
    cosf, sinf = _rope_tables(s)

    q, k_lat, v_lat = _qkv_project(x, mod, 0, None, gmix[0], w_qkv, qg, kg, cosf, sinf,
                                   n_q_heads=n_heads, n_kv_heads=n_kv, rope=True, ts=t["ts"])
    n_ctx = ctx.shape[1]
    k_ctx, v_ctx = _qkv_project(ctx, mod, 0, 4, gmix[0], w_qkv, qg, kg, cosf, sinf,
                                n_q_heads=0, n_kv_heads=n_kv, rope=False, ts=n_ctx)
    o, (w_ao, w_fo, w1, w2) = _attention(q, k_lat, v_lat, k_ctx, v_ctx,
                                         [attn_w_o[0], fourier_w_o[0], mlp_w1, mlp_w2], tq=t["tq"], tkv=t["tkv"])
    x1, h = _out_project(o, w_ao, x, mod, 0, gmlp[0], tm=t["tm_o"])
    x2, h = _mlp(h, w1, w2, x1, mod, 0, gmix[1], 1, tm=t["tm"], tf=t["tf"])

    y = _fourier_mix(h, tk=t["tk"], nsub=t["nsub"], tn=t["tn"])
    x1, h = _out_project(y, w_fo, x2, mod, 1, gmlp[1], tm=t["tm_o"])
    (out,) = _mlp(h, w1, w2, x1, mod, 1, final_norm_g.reshape(1, d), None, tm=t["tm"], tf=t["tf"])
    return out
```

```python
import functools

import numpy as np
import jax
import jax.numpy as jnp
from jax import lax
from jax.experimental import pallas as pl
from jax.experimental.pallas import tpu as pltpu

HEAD_DIM = 128
KV_GROUP = 4
GRID_W = 64
ROPE_THETA = 10000.0
FOURIER_GROUP_DIM = 256
N_MOD = 6
EPS = 1e-6
RADIX = 8
SQRT_HALF = 0.7071067811865476
LOG2_E = 1.4426950408889634
ROW_SUB = 256
LANES = 128
BF16_SUBLANES = 16
VMEM_LIMIT_BYTES = 60 * 1024 * 1024

F32 = jnp.float32
BF16 = jnp.bfloat16


def _params(*sem):
    return pltpu.CompilerParams(dimension_semantics=sem, vmem_limit_bytes=VMEM_LIMIT_BYTES)


def _resident(shape, index_map):
    return pl.BlockSpec(shape, index_map, pipeline_mode=pl.Buffered(1))


def _rms_scale(x):
    return lax.rsqrt(jnp.mean(x * x, axis=-1, keepdims=True) + EPS)


def _mod_kernel(c_ref, w_ref, b_ref, o_ref):
    c = c_ref[...]
    s = c / (1.0 + jnp.exp(-c))
    o_ref[...] = jnp.dot(s, w_ref[...], preferred_element_type=F32) + b_ref[...]


def _modulation(c_rows, ada_w, ada_b, tn):
    depth, d, n = ada_w.shape
    return pl.pallas_call(
        _mod_kernel,
        grid=(depth, n // tn),
        in_specs=[
            pl.BlockSpec((8, d), lambda l, j: (0, 0)),
            pl.BlockSpec((None, d, tn), lambda l, j: (l, 0, j)),
            pl.BlockSpec((None, 1, tn), lambda l, j: (l, 0, j)),
        ],
        out_specs=pl.BlockSpec((None, 8, tn), lambda l, j: (l, 0, j)),
        out_shape=jax.ShapeDtypeStruct((depth, 8, n), F32),
        compiler_params=_params("parallel", "parallel"),
        name="adaln_modulation",
    )(c_rows, ada_w, ada_b.reshape(depth, 1, n))


def _mod_spec(d, layer, which, row=None):
    return pl.BlockSpec((None, None, None, 1, d),
                        lambda *g: (layer, g[0] if row is None else row, which, 0, 0))


def _qkv_kernel(x_ref, sh_ref, sc_ref, gn_ref, w_ref, qg_ref, kg_ref, cos_ref, sin_ref, *out_refs,
                n_q_heads, n_kv_heads, rope, q_scale, sub):
    if n_q_heads:
        q_ref, k_ref, v_ref = out_refs
    else:
        k_ref, v_ref = out_refs
    group_w = KV_GROUP * HEAD_DIM
    kw = n_kv_heads * HEAD_DIM
    gain_row = gn_ref[...] * (1.0 + sc_ref[...])
    for r in range(x_ref.shape[0] // sub):
        rows = slice(r * sub, (r + 1) * sub)
        x = x_ref[rows, :]
        h = (x * _rms_scale(x) * gain_row + sh_ref[...]).astype(BF16)
        if rope:
            cosf, sinf = cos_ref[rows, :], sin_ref[rows, :]
            even = (lax.broadcasted_iota(jnp.int32, cosf.shape, 1) % 2) == 0

        def head_norm_rope(t, gain):
            t = t * _rms_scale(t) * gain
            if rope:
                partner = jnp.where(even, pltpu.roll(t, HEAD_DIM - 1, 1), pltpu.roll(t, 1, 1))
                t = t * cosf + partner * sinf
            return t

        col = 0
        for g in range(n_q_heads // KV_GROUP):
            acc = jnp.dot(h, w_ref[:, col:col + group_w], preferred_element_type=F32)
            for i in range(KV_GROUP):
                t = head_norm_rope(acc[:, i * HEAD_DIM:(i + 1) * HEAD_DIM], qg_ref[...])
                q_ref[g * KV_GROUP + i, rows, :] = (t * q_scale).astype(BF16)
            col += group_w
        acc = jnp.dot(h, w_ref[:, col:col + kw], preferred_element_type=F32)
        for i in range(n_kv_heads):
            k_ref[i, rows, :] = head_norm_rope(acc[:, i * HEAD_DIM:(i + 1) * HEAD_DIM], kg_ref[...]).astype(BF16)
        col += kw
        acc = jnp.dot(h, w_ref[:, col:col + kw], preferred_element_type=F32)
        for i in range(n_kv_heads):
            v_ref[i, rows, :HEAD_DIM] = acc[:, i * HEAD_DIM:(i + 1) * HEAD_DIM].astype(BF16)
            v_ref[i, rows, HEAD_DIM:] = jnp.ones((sub, HEAD_DIM), BF16)


def _qkv_project(x, mod, layer, mod_row, gn, w, qg, kg, cosf, sinf, *, n_q_heads, n_kv_heads, rope, ts):
    b, s, d = x.shape
    ncols = (n_q_heads + 2 * n_kv_heads) * HEAD_DIM
    col_block = (w.shape[1] - ncols) // ncols
    assert col_block * ncols == w.shape[1] - ncols
    head_out = lambda nh, width=HEAD_DIM: pl.BlockSpec((None, nh, ts, width), lambda bi, i: (bi, 0, i, 0))
    head_shape = lambda nh, width=HEAD_DIM: jax.ShapeDtypeStruct((b, nh, s, width), BF16)
    out_specs = [head_out(n_kv_heads), head_out(n_kv_heads, 2 * HEAD_DIM)]
    out_shape = [head_shape(n_kv_heads), head_shape(n_kv_heads, 2 * HEAD_DIM)]
    if n_q_heads:
        out_specs.insert(0, head_out(n_q_heads))
        out_shape.insert(0, head_shape(n_q_heads))
    kernel = functools.partial(_qkv_kernel, n_q_heads=n_q_heads, n_kv_heads=n_kv_heads, rope=rope,
                               q_scale=HEAD_DIM ** -0.5 * LOG2_E, sub=min(ROW_SUB, ts))
    return pl.pallas_call(
        kernel,
        grid=(b, s // ts),
        in_specs=[
            pl.BlockSpec((None, ts, d), lambda bi, i: (bi, i, 0)),
            _mod_spec(d, layer, 0, mod_row),
            _mod_spec(d, layer, 1, mod_row),
            pl.BlockSpec((1, d), lambda bi, i: (0, 0)),
            _resident((d, ncols), lambda bi, i: (0, col_block)),
            pl.BlockSpec((1, HEAD_DIM), lambda bi, i: (0, 0)),
            pl.BlockSpec((1, HEAD_DIM), lambda bi, i: (0, 0)),
            pl.BlockSpec((ts, HEAD_DIM), lambda bi, i: (i, 0)),
            pl.BlockSpec((ts, HEAD_DIM), lambda bi, i: (i, 0)),
        ],
        out_specs=out_specs,
        out_shape=out_shape,
        compiler_params=_params("parallel", "parallel"),
        name="qkv_project" if n_q_heads else "kv_project_ctx",
    )(x, mod, mod, gn, w, qg, kg, cosf, sinf)


def _attn_kernel(q_ref, kl_ref, vl_ref, kc_ref, vc_ref, *refs, tkv, n_cast):
    cast_in, o_ref, cast_out = refs[:n_cast], refs[n_cast], refs[n_cast + 1:]
    for src, dst in zip(cast_in, cast_out):
        dst[...] = src[...].astype(dst.dtype)
    g, tq, hd = q_ref.shape
    q = q_ref[...].reshape(g * tq, hd)
    chunks = [(kl_ref, vl_ref, i * tkv, tkv) for i in range(kl_ref.shape[0] // tkv)]
    chunks.append((kc_ref, vc_ref, 0, kc_ref.shape[0]))
    m = acc = None
    for k_ref, v_ref, off, n in chunks:
        s = lax.dot_general(q, k_ref[off:off + n, :], (((1,), (1,)), ((), ())), preferred_element_type=F32)
        m_chunk = jnp.max(s, axis=-1, keepdims=True)
        m_new = m_chunk if m is None else jnp.maximum(m, m_chunk)
        pv = jnp.dot(jnp.exp2(s - m_new).astype(BF16), v_ref[off:off + n, :], preferred_element_type=F32)
        acc = pv if m is None else jnp.exp2(m - m_new) * acc + pv
        m = m_new
    o = acc[:, :hd] / acc[:, hd:]
    for i in range(g):
        o_ref[:, i * hd:(i + 1) * hd] = o[i * tq:(i + 1) * tq].astype(o_ref.dtype)


def _attention(q, k_lat, v_lat, k_ctx, v_ctx, weights, *, tq, tkv):
    b, h, s, hd = q.shape
    kvh = h // KV_GROUP
    n_ctx = k_ctx.shape[2]
    n_i = s // tq
    steps = b * kvh * n_i
    kv_spec = lambda n, width=hd: pl.BlockSpec((None, None, n, width), lambda bi, j, i: (bi, j, 0, 0))
    sliced = [w.reshape(steps, -1, w.shape[-1]) for w in weights]
    for w in sliced:
        assert w.shape[1] % BF16_SUBLANES == 0
    cast_specs = [pl.BlockSpec((None,) + w.shape[1:], lambda bi, j, i: ((bi * kvh + j) * n_i + i, 0, 0))
                  for w in sliced]
    out = pl.pallas_call(
        functools.partial(_attn_kernel, tkv=tkv, n_cast=len(sliced)),
        grid=(b, kvh, n_i),
        in_specs=[
            pl.BlockSpec((None, KV_GROUP, tq, hd), lambda bi, j, i: (bi, j, i, 0)),
            kv_spec(s), kv_spec(s, 2 * hd), kv_spec(n_ctx), kv_spec(n_ctx, 2 * hd),
        ] + cast_specs,
        out_specs=[pl.BlockSpec((None, tq, KV_GROUP * hd), lambda bi, j, i: (bi, i, j))] + cast_specs,
        out_shape=[jax.ShapeDtypeStruct((b, s, h * hd), BF16)]
        + [jax.ShapeDtypeStruct(w.shape, BF16) for w in sliced],
        compiler_params=_params("parallel", "parallel", "parallel"),
        name="gqa_attention",
    )(q, k_lat, v_lat, k_ctx, v_ctx, *sliced)
    return out[0], [c.reshape(w.shape) for c, w in zip(out[1:], weights)]


def _oproj_kernel(a_ref, w_ref, x_ref, g_ref, gn_ref, sc_ref, sh_ref, x1_ref, h_ref, *, sub):
    gain_row = gn_ref[...] * (1.0 + sc_ref[...])
    for r in range(a_ref.shape[0] // sub):
        rows = slice(r * sub, (r + 1) * sub)
        y = jnp.dot(a_ref[rows, :], w_ref[...], preferred_element_type=F32)
        x1 = x_ref[rows, :] + g_ref[...] * y
        x1_ref[rows, :] = x1
        h_ref[rows, :] = (x1 * _rms_scale(x1) * gain_row + sh_ref[...]).astype(BF16)


def _out_project(a, w, x, mod, layer, gn, *, tm):
    b, s, d = x.shape
    tile = pl.BlockSpec((None, tm, d), lambda bi, i: (bi, i, 0))
    return pl.pallas_call(
        functools.partial(_oproj_kernel, sub=min(ROW_SUB, tm)),
        grid=(b, s // tm),
        in_specs=[
            tile,
            _resident((d, d), lambda bi, i: (0, 0)),
            tile,
            _mod_spec(d, layer, 2),
            pl.BlockSpec((1, d), lambda bi, i: (0, 0)),
            _mod_spec(d, layer, 4),
            _mod_spec(d, layer, 3),
        ],
        out_specs=[tile, tile],
        out_shape=[jax.ShapeDtypeStruct((b, s, d), F32), jax.ShapeDtypeStruct((b, s, d), BF16)],
        compiler_params=_params("parallel", "parallel"),
        name="out_project",
    )(a, w, x, mod, gn, mod, mod)


def _mlp_kernel(h_ref, w1_ref, w2_ref, x1_ref, g2_ref, *refs, final, sub):
    c = pl.program_id(2)
    last = pl.num_programs(2) - 1
    if final:
        gn_ref, out_ref, acc_ref = refs
    else:
        x2_ref, r_ref, acc_ref = refs

    def chunk_sum(rows):
        u = jnp.maximum(jnp.dot(h_ref[rows, :], w1_ref[...], preferred_element_type=F32), 0.0)
        return jnp.dot((u * u).astype(BF16), w2_ref[...], preferred_element_type=F32)

    everything = slice(None)

    @pl.when(c == 0)
    def _():
        acc_ref[...] = chunk_sum(everything)

    @pl.when(jnp.logical_and(c > 0, c < last))
    def _():
        acc_ref[...] += chunk_sum(everything)

    @pl.when(c == last)
    def _():
        for r in range(h_ref.shape[0] // sub):
            rows = slice(r * sub, (r + 1) * sub)
            x2 = x1_ref[rows, :] + g2_ref[...] * (acc_ref[rows, :] + chunk_sum(rows))
            scale = _rms_scale(x2)
            if final:
                out_ref[rows, :] = x2 * scale * gn_ref[...]
            else:
                x2_ref[rows, :] = x2
                r_ref[rows, :] = jnp.broadcast_to(scale, (sub, LANES))


def _mlp(h, w1, w2, x1, mod, layer, gn_final, *, tm, tf):
    b, s, d = x1.shape
    f = w1.shape[2]
    assert f // tf >= 2
    final = gn_final is not None
    tile = pl.BlockSpec((None, tm, d), lambda bi, i, c: (bi, i, 0))
    out_shape = [jax.ShapeDtypeStruct((b, s, d), F32)]
    out_specs = [tile]
    extra_specs, extra_args = [], []
    if final:
        extra_specs.append(pl.BlockSpec((1, d), lambda bi, i, c: (0, 0)))
        extra_args.append(gn_final)
    else:
        out_shape.append(jax.ShapeDtypeStruct((b, s, LANES), F32))
        out_specs.append(pl.BlockSpec((None, tm, LANES), lambda bi, i, c: (bi, i, 0)))
    return pl.pallas_call(
        functools.partial(_mlp_kernel, final=final, sub=min(ROW_SUB, tm)),
        grid=(b, s // tm, f // tf),
        in_specs=[
            tile,
            pl.BlockSpec((None, d, tf), lambda bi, i, c: (layer, 0, c)),
            pl.BlockSpec((None, tf, d), lambda bi, i, c: (layer, c, 0)),
            tile,
            _mod_spec(d, layer, 5),
        ] + extra_specs,
        out_specs=out_specs,
        out_shape=out_shape,
        scratch_shapes=[pltpu.VMEM((tm, d), F32)],
        compiler_params=_params("parallel", "parallel", "arbitrary"),
        name="mlp_final" if final else "mlp",
    )(h, w1, w2, x1, mod, *extra_args)


def _seq_dft_mats(s, tk):
    n = s // RADIX
    k1 = np.arange(n, dtype=np.int64)[:, None]
    m = np.arange(n, dtype=np.int64)[None, :]
    mats = []
    for j in range(RADIX):
        ang = 2.0 * np.pi * (((RADIX * m + j) * k1) % s) / s
        c = (np.cos(ang) / np.sqrt(s)).reshape(n // tk, tk, n)
        sn = (-np.sin(ang) / np.sqrt(s)).reshape(n // tk, tk, n)
        mats.append(np.concatenate([c, sn], axis=1))
    return np.stack(mats).astype(np.float32)


def _chan_dft_mat():
    dg = FOURIER_GROUP_DIM
    idx = np.arange(dg, dtype=np.int64)
    ang = 2.0 * np.pi * ((idx[:, None] * idx[None, :]) % dg) / dg
    return (np.concatenate([np.cos(ang), np.sin(ang)], axis=0) / np.sqrt(dg)).astype(np.float32)


def _dft4(z):
    (ar, ai), (br, bi), (cr, ci), (dr, di) = z
    er, ei, fr, fi = ar + cr, ai + ci, ar - cr, ai - ci
    gr, gi, hr, hi = br + dr, bi + di, br - dr, bi - di
    return [(er + gr, ei + gi), (fr + hi, fi - hr), (er - gr, ei - gi), (fr - hi, fi + hr)]


def _fourier_kernel(*refs, n_slab):
    x_refs, (r_ref, gn_ref, sc_ref, sh_ref, m_ref, cs_ref, o_ref) = refs[:n_slab], refs[n_slab:]
    nsub, tk = m_ref.shape[1], m_ref.shape[2] // 2
    n = m_ref.shape[3]
    dg = FOURIER_GROUP_DIM
    slabs_per_group = dg // LANES
    cs = cs_ref[...]
    gain_row = gn_ref[...] * (1.0 + sc_ref[...])
    shift_row = sh_ref[...]
    scales = [r_ref[pl.ds(j, n, stride=RADIX), :] for j in range(RADIX)]
    for gi in range(n_slab // slabs_per_group):
        planes = []
        for j in range(RADIX):
            cols = []
            for q in range(gi * slabs_per_group, (gi + 1) * slabs_per_group):
                lanes = slice(q * LANES, (q + 1) * LANES)
                xq = x_refs[q][pl.ds(j, n, stride=RADIX), :]
                cols.append((xq * scales[j] * gain_row[:, lanes] + shift_row[:, lanes]).astype(BF16))
            planes.append(jnp.concatenate(cols, axis=1))
        for r in range(nsub):
            y = []
            for j in range(RADIX):
                yj = jnp.dot(m_ref[j, r], planes[j], preferred_element_type=F32)
                y.append((yj[:tk], yj[tk:]))
            half = RADIX // 2
            u = [(y[j][0] + y[j + half][0], y[j][1] + y[j + half][1]) for j in range(half)]
            v = [(y[j][0] - y[j + half][0], y[j][1] - y[j + half][1]) for j in range(half)]
            w = [v[0],
                 ((v[1][0] + v[1][1]) * SQRT_HALF, (v[1][1] - v[1][0]) * SQRT_HALF),
                 (v[2][1], -v[2][0]),
                 ((v[3][1] - v[3][0]) * SQRT_HALF, (-v[3][0] - v[3][1]) * SQRT_HALF)]
            even, odd = _dft4(u), _dft4(w)
            for k2 in range(RADIX):
                pr, pi = (even if k2 % 2 == 0 else odd)[k2 // 2]
                lhs = jnp.concatenate([pr.astype(BF16), pi.astype(BF16)], axis=1)
                out = jnp.dot(lhs, cs, preferred_element_type=F32)
                o_ref[k2, r * tk:(r + 1) * tk, gi * dg:(gi + 1) * dg] = out.astype(o_ref.dtype)


def _fourier_mix(x, row_scale, mod, layer, gn, *, tk, nsub, tn):
    b, s, d = x.shape
    n = s // RADIX
    n_slab = tn // LANES
    mats = jnp.asarray(_seq_dft_mats(s, tk)).astype(BF16)
    cs = jnp.asarray(_chan_dft_mat()).astype(BF16)
    x_specs = [pl.BlockSpec((None, s, LANES), lambda bi, c, t, q=q: (bi, 0, c * n_slab + q)) for q in range(n_slab)]
    mod_cols = lambda which: pl.BlockSpec((None, None, None, 1, tn), lambda bi, c, t: (layer, bi, which, 0, c))
    y = pl.pallas_call(
        functools.partial(_fourier_kernel, n_slab=n_slab),
        grid=(b, d // tn, n // (tk * nsub)),
        in_specs=x_specs + [
            pl.BlockSpec((None, s, LANES), lambda bi, c, t: (bi, 0, 0)),
            pl.BlockSpec((1, tn), lambda bi, c, t: (0, c)),
            mod_cols(1),
            mod_cols(0),
            pl.BlockSpec((RADIX, nsub, 2 * tk, n), lambda bi, c, t: (0, t, 0, 0)),
            pl.BlockSpec((2 * FOURIER_GROUP_DIM, FOURIER_GROUP_DIM), lambda bi, c, t: (0, 0)),
        ],
        out_specs=pl.BlockSpec((None, RADIX, tk * nsub, tn), lambda bi, c, t: (bi, 0, t, c)),
        out_shape=jax.ShapeDtypeStruct((b, RADIX, n, d), BF16),
        compiler_params=_params("parallel", "parallel", "parallel"),
        name="fourier_mix",
    )(*([x] * n_slab), row_scale, gn, mod, mod, mats, cs)
    return y.reshape(b, s, d)


def _rope_tables(s):
    rows = s // GRID_W
    row = jnp.broadcast_to(jnp.arange(rows)[:, None], (rows, GRID_W)).reshape(-1)
    col = jnp.broadcast_to(jnp.arange(GRID_W)[None, :], (rows, GRID_W)).reshape(-1)
    n_freq = HEAD_DIM // 4
    inv_freq = ROPE_THETA ** (-jnp.arange(n_freq, dtype=F32) / n_freq)
    ang = jnp.concatenate([row.astype(F32)[:, None] * inv_freq, col.astype(F32)[:, None] * inv_freq], axis=-1)
    cos, sin = jnp.cos(ang), jnp.sin(ang)
    cosf = jnp.repeat(cos, 2, axis=-1)
    sinf = jnp.stack([-sin, sin], axis=-1).reshape(s, HEAD_DIM)
    return cosf, sinf


def _tiles(s, d, f):
    return dict(
        ts=min(1024, s), tq=min(512, s), tkv=min(256, s), tm_o=min(512, s), tm=min(512, s), tf=min(1024, f),
        tk=min(ROW_SUB, s // RADIX), nsub=max(1, min(2, s // RADIX // ROW_SUB)), tn=min(512, d),
    )


def kernel(x, c, ctx, c_ctx, ada_w, ada_b, norm_mix_g, norm_mlp_g, attn_w_qkv, attn_q_norm_g, attn_k_norm_g,
           attn_w_o, fourier_w_o, mlp_w1, mlp_w2, final_norm_g):
    b, s, d = x.shape
    depth = ada_w.shape[0]
    assert depth == 2 and d % (KV_GROUP * HEAD_DIM) == 0 and s % GRID_W == 0 and b <= 4
    n_heads = d // HEAD_DIM
    n_kv = n_heads // KV_GROUP
    f = mlp_w1.shape[2]
    t = _tiles(s, d, f)

    c_rows = jnp.zeros((8, d), F32).at[:b].set(c).at[4].set(c_ctx)
    mod = _modulation(c_rows, ada_w, ada_b, tn=min(1024, N_MOD * d)).reshape(depth, 8, N_MOD, 1, d)

    w_qkv = attn_w_qkv[0].astype(BF16)
    gmix = norm_mix_g.reshape(depth, 1, d)
    gmlp = norm_mlp_g.reshape(depth, 1, d)
    qg = attn_q_norm_g[0].reshape(1, HEAD_DIM)
    kg = attn_k_norm_g[0].reshape(1, HEAD_DIM)
    cosf, sinf = _rope_tables(s)

    q, k_lat, v_lat = _qkv_project(x, mod, 0, None, gmix[0], w_qkv, qg, kg, cosf, sinf,
                                   n_q_heads=n_heads, n_kv_heads=n_kv, rope=True, ts=t["ts"])
    n_ctx = ctx.shape[1]
    k_ctx, v_ctx = _qkv_project(ctx, mod, 0, 4, gmix[0], w_qkv, qg, kg, cosf, sinf,
                                n_q_heads=0, n_kv_heads=n_kv, rope=False, ts=n_ctx)
    o, (w_ao, w_fo, w1, w2) = _attention(q, k_lat, v_lat, k_ctx, v_ctx,
                                         [attn_w_o[0], fourier_w_o[0], mlp_w1, mlp_w2], tq=t["tq"], tkv=t["tkv"])
    x1, h = _out_project(o, w_ao, x, mod, 0, gmlp[0], tm=t["tm_o"])
    x2, row_scale = _mlp(h, w1, w2, x1, mod, 0, None, tm=t["tm"], tf=t["tf"])

    y = _fourier_mix(x2, row_scale, mod, 1, gmix[1], tk=t["tk"], nsub=t["nsub"], tn=t["tn"])
    x1, h = _out_project(y, w_fo, x2, mod, 1, gmlp[1], tm=t["tm_o"])
    (out,) = _mlp(h, w1, w2, x1, mod, 1, final_norm_g.reshape(1, d), tm=t["tm"], tf=t["tf"])
    return out
```

```python
import functools

import numpy as np
import jax
import jax.numpy as jnp
from jax import lax
from jax.experimental import pallas as pl
from jax.experimental.pallas import tpu as pltpu

HEAD_DIM = 128
KV_GROUP = 4
GRID_W = 64
ROPE_THETA = 10000.0
FOURIER_GROUP_DIM = 256
N_MOD = 6
EPS = 1e-6
RADIX = 8
SQRT_HALF = 0.7071067811865476
LOG2_E = 1.4426950408889634
ROW_SUB = 256
LANES = 128
BF16_SUBLANES = 16
VMEM_LIMIT_BYTES = 60 * 1024 * 1024

F32 = jnp.float32
BF16 = jnp.bfloat16


def _params(*sem):
    return pltpu.CompilerParams(dimension_semantics=sem, vmem_limit_bytes=VMEM_LIMIT_BYTES)


def _resident(shape, index_map):
    return pl.BlockSpec(shape, index_map, pipeline_mode=pl.Buffered(1))


def _rms_scale(x):
    return lax.rsqrt(jnp.mean(x * x, axis=-1, keepdims=True) + EPS)


def _mod_kernel(c_ref, w_ref, b_ref, o_ref):
    c = c_ref[...]
    s = c / (1.0 + jnp.exp(-c))
    o_ref[...] = jnp.dot(s, w_ref[...], preferred_element_type=F32) + b_ref[...]


def _modulation(c_rows, ada_w, ada_b, tn):
    depth, d, n = ada_w.shape
    return pl.pallas_call(
        _mod_kernel,
        grid=(depth, n // tn),
        in_specs=[
            pl.BlockSpec((8, d), lambda l, j: (0, 0)),
            pl.BlockSpec((None, d, tn), lambda l, j: (l, 0, j)),
            pl.BlockSpec((None, 1, tn), lambda l, j: (l, 0, j)),
        ],
        out_specs=pl.BlockSpec((None, 8, tn), lambda l, j: (l, 0, j)),
        out_shape=jax.ShapeDtypeStruct((depth, 8, n), F32),
        compiler_params=_params("parallel", "parallel"),
        name="adaln_modulation",
    )(c_rows, ada_w, ada_b.reshape(depth, 1, n))


def _mod_spec(d, layer, which, row=None):
    return pl.BlockSpec((None, None, None, 1, d),
                        lambda *g: (layer, g[0] if row is None else row, which, 0, 0))


def _qkv_kernel(x_ref, sh_ref, sc_ref, gn_ref, w_ref, qg_ref, kg_ref, cos_ref, sin_ref, *out_refs,
                n_q_heads, n_kv_heads, rope, q_scale, sub):
    if n_q_heads:
        q_ref, k_ref, v_ref = out_refs
    else:
        k_ref, v_ref = out_refs
    group_w = KV_GROUP * HEAD_DIM
    kw = n_kv_heads * HEAD_DIM
    gain_row = gn_ref[...] * (1.0 + sc_ref[...])
    for r in range(x_ref.shape[0] // sub):
        rows = slice(r * sub, (r + 1) * sub)
        x = x_ref[rows, :]
        h = (x * _rms_scale(x) * gain_row + sh_ref[...]).astype(BF16)
        if rope:
            cosf, sinf = cos_ref[rows, :], sin_ref[rows, :]
            even = (lax.broadcasted_iota(jnp.int32, cosf.shape, 1) % 2) == 0

        def head_norm_rope(t, gain):
            t = t * _rms_scale(t) * gain
            if rope:
                partner = jnp.where(even, pltpu.roll(t, HEAD_DIM - 1, 1), pltpu.roll(t, 1, 1))
                t = t * cosf + partner * sinf
            return t

        col = 0
        for g in range(n_q_heads // KV_GROUP):
            acc = jnp.dot(h, w_ref[:, col:col + group_w], preferred_element_type=F32)
            for i in range(KV_GROUP):
                t = head_norm_rope(acc[:, i * HEAD_DIM:(i + 1) * HEAD_DIM], qg_ref[...])
                q_ref[g * KV_GROUP + i, rows, :] = (t * q_scale).astype(BF16)
            col += group_w
        acc = jnp.dot(h, w_ref[:, col:col + kw], preferred_element_type=F32)
        for i in range(n_kv_heads):
            k_ref[i, rows, :] = head_norm_rope(acc[:, i * HEAD_DIM:(i + 1) * HEAD_DIM], kg_ref[...]).astype(BF16)
        col += kw
        acc = jnp.dot(h, w_ref[:, col:col + kw], preferred_element_type=F32)
        for i in range(n_kv_heads):
            v_ref[i, rows, :HEAD_DIM] = acc[:, i * HEAD_DIM:(i + 1) * HEAD_DIM].astype(BF16)
            v_ref[i, rows, HEAD_DIM:] = jnp.ones((sub, HEAD_DIM), BF16)


def _qkv_project(x, mod, layer, mod_row, gn, w, qg, kg, cosf, sinf, *, n_q_heads, n_kv_heads, rope, ts):
    b, s, d = x.shape
    ncols = (n_q_heads + 2 * n_kv_heads) * HEAD_DIM
    col_block = (w.shape[1] - ncols) // ncols
    assert col_block * ncols == w.shape[1] - ncols
    head_out = lambda nh, width=HEAD_DIM: pl.BlockSpec((None, nh, ts, width), lambda bi, i: (bi, 0, i, 0))
    head_shape = lambda nh, width=HEAD_DIM: jax.ShapeDtypeStruct((b, nh, s, width), BF16)
    out_specs = [head_out(n_kv_heads), head_out(n_kv_heads, 2 * HEAD_DIM)]
    out_shape = [head_shape(n_kv_heads), head_shape(n_kv_heads, 2 * HEAD_DIM)]
    if n_q_heads:
        out_specs.insert(0, head_out(n_q_heads))
        out_shape.insert(0, head_shape(n_q_heads))
    kernel = functools.partial(_qkv_kernel, n_q_heads=n_q_heads, n_kv_heads=n_kv_heads, rope=rope,
                               q_scale=HEAD_DIM ** -0.5 * LOG2_E, sub=min(ROW_SUB, ts))
    return pl.pallas_call(
        kernel,
        grid=(b, s // ts),
        in_specs=[
            pl.BlockSpec((None, ts, d), lambda bi, i: (bi, i, 0)),
            _mod_spec(d, layer, 0, mod_row),
            _mod_spec(d, layer, 1, mod_row),
            pl.BlockSpec((1, d), lambda bi, i: (0, 0)),
            _resident((d, ncols), lambda bi, i: (0, col_block)),
            pl.BlockSpec((1, HEAD_DIM), lambda bi, i: (0, 0)),
            pl.BlockSpec((1, HEAD_DIM), lambda bi, i: (0, 0)),
            pl.BlockSpec((ts, HEAD_DIM), lambda bi, i: (i, 0)),
            pl.BlockSpec((ts, HEAD_DIM), lambda bi, i: (i, 0)),
        ],
        out_specs=out_specs,
        out_shape=out_shape,
        compiler_params=_params("parallel", "parallel"),
        name="qkv_project" if n_q_heads else "kv_project_ctx",
    )(x, mod, mod, gn, w, qg, kg, cosf, sinf)


def _attn_kernel(q_ref, kl_ref, vl_ref, kc_ref, vc_ref, *refs, tkv, n_cast):
    cast_in, o_ref, cast_out = refs[:n_cast], refs[n_cast], refs[n_cast + 1:]
    for src, dst in zip(cast_in, cast_out):
        dst[...] = src[...].astype(dst.dtype)
    g, tq, hd = q_ref.shape
    q = q_ref[...].reshape(g * tq, hd)
    chunks = [(kl_ref, vl_ref, i * tkv, tkv) for i in range(kl_ref.shape[0] // tkv)]
    chunks.append((kc_ref, vc_ref, 0, kc_ref.shape[0]))
    m = acc = None
    for k_ref, v_ref, off, n in chunks:
        s = lax.dot_general(q, k_ref[off:off + n, :], (((1,), (1,)), ((), ())), preferred_element_type=F32)
        m_chunk = jnp.max(s, axis=-1, keepdims=True)
        m_new = m_chunk if m is None else jnp.maximum(m, m_chunk)
        pv = jnp.dot(jnp.exp2(s - m_new).astype(BF16), v_ref[off:off + n, :], preferred_element_type=F32)
        acc = pv if m is None else jnp.exp2(m - m_new) * acc + pv
        m = m_new
    o = acc[:, :hd] / acc[:, hd:]
    for i in range(g):
        o_ref[:, i * hd:(i + 1) * hd] = o[i * tq:(i + 1) * tq].astype(o_ref.dtype)


def _attention(q, k_lat, v_lat, k_ctx, v_ctx, weights, *, tq, tkv):
    b, h, s, hd = q.shape
    kvh = h // KV_GROUP
    n_ctx = k_ctx.shape[2]
    n_i = s // tq
    steps = b * kvh * n_i
    kv_spec = lambda n, width=hd: pl.BlockSpec((None, None, n, width), lambda bi, j, i: (bi, j, 0, 0))
    sliced = [w.reshape(steps, -1, w.shape[-1]) for w in weights]
    for w in sliced:
        assert w.shape[1] % BF16_SUBLANES == 0
    cast_specs = [pl.BlockSpec((None,) + w.shape[1:], lambda bi, j, i: ((bi * kvh + j) * n_i + i, 0, 0))
                  for w in sliced]
    out = pl.pallas_call(
        functools.partial(_attn_kernel, tkv=tkv, n_cast=len(sliced)),
        grid=(b, kvh, n_i),
        in_specs=[
            pl.BlockSpec((None, KV_GROUP, tq, hd), lambda bi, j, i: (bi, j, i, 0)),
            kv_spec(s), kv_spec(s, 2 * hd), kv_spec(n_ctx), kv_spec(n_ctx, 2 * hd),
        ] + cast_specs,
        out_specs=[pl.BlockSpec((None, tq, KV_GROUP * hd), lambda bi, j, i: (bi, i, j))] + cast_specs,
        out_shape=[jax.ShapeDtypeStruct((b, s, h * hd), BF16)]
        + [jax.ShapeDtypeStruct(w.shape, BF16) for w in sliced],
        compiler_params=_params("parallel", "parallel", "parallel"),
        name="gqa_attention",
    )(q, k_lat, v_lat, k_ctx, v_ctx, *sliced)
    return out[0], [c.reshape(w.shape) for c, w in zip(out[1:], weights)]


def _oproj_kernel(a_ref, w_ref, x_ref, g_ref, gn_ref, sc_ref, sh_ref, x1_ref, h_ref, *, sub):
    gain_row = gn_ref[...] * (1.0 + sc_ref[...])
    for r in range(a_ref.shape[0] // sub):
        rows = slice(r * sub, (r + 1) * sub)
        y = jnp.dot(a_ref[rows, :], w_ref[...], preferred_element_type=F32)
        x1 = x_ref[rows, :] + g_ref[...] * y
        x1_ref[rows, :] = x1
        h_ref[rows, :] = (x1 * _rms_scale(x1) * gain_row + sh_ref[...]).astype(BF16)


def _out_project(a, w, x, mod, layer, gn, *, tm):
    b, s, d = x.shape
    tile = pl.BlockSpec((None, tm, d), lambda bi, i: (bi, i, 0))
    return pl.pallas_call(
        functools.partial(_oproj_kernel, sub=min(ROW_SUB, tm)),
        grid=(b, s // tm),
        in_specs=[
            tile,
            _resident((d, d), lambda bi, i: (0, 0)),
            tile,
            _mod_spec(d, layer, 2),
            pl.BlockSpec((1, d), lambda bi, i: (0, 0)),
            _mod_spec(d, layer, 4),
            _mod_spec(d, layer, 3),
        ],
        out_specs=[tile, tile],
        out_shape=[jax.ShapeDtypeStruct((b, s, d), F32), jax.ShapeDtypeStruct((b, s, d), BF16)],
        compiler_params=_params("parallel", "parallel"),
        name="out_project",
    )(a, w, x, mod, gn, mod, mod)


def _mlp_kernel(h_ref, w1_ref, w2_ref, x1_ref, g2_ref, *refs, final, sub):
    c = pl.program_id(2)
    last = pl.num_programs(2) - 1
    if final:
        gn_ref, out_ref = refs
    else:
        out_ref, r_ref = refs
    row_blocks = [slice(r * sub, (r + 1) * sub) for r in range(h_ref.shape[0] // sub)]

    def chunk_sum(rows):
        u = jnp.maximum(jnp.dot(h_ref[rows, :], w1_ref[...], preferred_element_type=F32), 0.0)
        return jnp.dot((u * u).astype(BF16), w2_ref[...], preferred_element_type=F32)

    @pl.when(c == 0)
    def _():
        for rows in row_blocks:
            out_ref[rows, :] = chunk_sum(rows)

    @pl.when(jnp.logical_and(c > 0, c < last))
    def _():
        for rows in row_blocks:
            out_ref[rows, :] += chunk_sum(rows)

    @pl.when(c == last)
    def _():
        for rows in row_blocks:
            x2 = x1_ref[rows, :] + g2_ref[...] * (out_ref[rows, :] + chunk_sum(rows))
            scale = _rms_scale(x2)
            if final:
                out_ref[rows, :] = x2 * scale * gn_ref[...]
            else:
                out_ref[rows, :] = x2
                r_ref[rows, :] = jnp.broadcast_to(scale, (sub, LANES))


def _mlp(h, w1, w2, x1, mod, layer, gn_final, *, tm, tf):
    b, s, d = x1.shape
    f = w1.shape[2]
    assert f // tf >= 2
    final = gn_final is not None
    tile = pl.BlockSpec((None, tm, d), lambda bi, i, c: (bi, i, 0))
    out_shape = [jax.ShapeDtypeStruct((b, s, d), F32)]
    out_specs = [tile]
    extra_specs, extra_args = [], []
    if final:
        extra_specs.append(pl.BlockSpec((1, d), lambda bi, i, c: (0, 0)))
        extra_args.append(gn_final)
    else:
        out_shape.append(jax.ShapeDtypeStruct((b, s, LANES), F32))
        out_specs.append(pl.BlockSpec((None, tm, LANES), lambda bi, i, c: (bi, i, 0)))
    return pl.pallas_call(
        functools.partial(_mlp_kernel, final=final, sub=min(ROW_SUB, tm)),
        grid=(b, s // tm, f // tf),
        in_specs=[
            tile,
            pl.BlockSpec((None, d, tf), lambda bi, i, c: (layer, 0, c)),
            pl.BlockSpec((None, tf, d), lambda bi, i, c: (layer, c, 0)),
            tile,
            _mod_spec(d, layer, 5),
        ] + extra_specs,
        out_specs=out_specs,
        out_shape=out_shape,
        compiler_params=_params("parallel", "parallel", "arbitrary"),
        name="mlp_final" if final else "mlp",
    )(h, w1, w2, x1, mod, *extra_args)


def _seq_dft_mats(s, tk):
    n = s // RADIX
    k1 = np.arange(n, dtype=np.int64)[:, None]
    m = np.arange(n, dtype=np.int64)[None, :]
    mats = []
    for j in range(RADIX):
        ang = 2.0 * np.pi * (((RADIX * m + j) * k1) % s) / s
        c = (np.cos(ang) / np.sqrt(s)).reshape(n // tk, tk, n)
        sn = (-np.sin(ang) / np.sqrt(s)).reshape(n // tk, tk, n)
        mats.append(np.concatenate([c, sn], axis=1))
    return np.stack(mats).astype(np.float32)


def _chan_dft_mat():
    dg = FOURIER_GROUP_DIM
    idx = np.arange(dg, dtype=np.int64)
    ang = 2.0 * np.pi * ((idx[:, None] * idx[None, :]) % dg) / dg
    return (np.concatenate([np.cos(ang), np.sin(ang)], axis=0) / np.sqrt(dg)).astype(np.float32)


def _dft4(z):
    (ar, ai), (br, bi), (cr, ci), (dr, di) = z
    er, ei, fr, fi = ar + cr, ai + ci, ar - cr, ai - ci
    gr, gi, hr, hi = br + dr, bi + di, br - dr, bi - di
    return [(er + gr, ei + gi), (fr + hi, fi - hr), (er - gr, ei - gi), (fr - hi, fi + hr)]


def _fourier_kernel(*refs, n_slab):
    x_refs, (r_ref, gn_ref, sc_ref, sh_ref, m_ref, cs_ref, o_ref) = refs[:n_slab], refs[n_slab:]
    nsub, tk = m_ref.shape[1], m_ref.shape[2] // 2
    n = m_ref.shape[3]
    dg = FOURIER_GROUP_DIM
    slabs_per_group = dg // LANES
    cs = cs_ref[...]
    gain_row = gn_ref[...] * (1.0 + sc_ref[...])
    shift_row = sh_ref[...]
    scales = [r_ref[pl.ds(j, n, stride=RADIX), :] for j in range(RADIX)]
    for gi in range(n_slab // slabs_per_group):
        planes = []
        for j in range(RADIX):
            cols = []
            for q in range(gi * slabs_per_group, (gi + 1) * slabs_per_group):
                lanes = slice(q * LANES, (q + 1) * LANES)
                xq = x_refs[q][pl.ds(j, n, stride=RADIX), :]
                cols.append((xq * scales[j] * gain_row[:, lanes] + shift_row[:, lanes]).astype(BF16))
            planes.append(jnp.concatenate(cols, axis=1))
        for r in range(nsub):
            y = []
            for j in range(RADIX):
                yj = jnp.dot(m_ref[j, r], planes[j], preferred_element_type=F32)
                y.append((yj[:tk], yj[tk:]))
            half = RADIX // 2
            u = [(y[j][0] + y[j + half][0], y[j][1] + y[j + half][1]) for j in range(half)]
            v = [(y[j][0] - y[j + half][0], y[j][1] - y[j + half][1]) for j in range(half)]
            w = [v[0],
                 ((v[1][0] + v[1][1]) * SQRT_HALF, (v[1][1] - v[1][0]) * SQRT_HALF),
                 (v[2][1], -v[2][0]),
                 ((v[3][1] - v[3][0]) * SQRT_HALF, (-v[3][0] - v[3][1]) * SQRT_HALF)]
            even, odd = _dft4(u), _dft4(w)
            for k2 in range(RADIX):
                pr, pi = (even if k2 % 2 == 0 else odd)[k2 // 2]
                lhs = jnp.concatenate([pr.astype(BF16), pi.astype(BF16)], axis=1)
                out = jnp.dot(lhs, cs, preferred_element_type=F32)
                o_ref[k2, r * tk:(r + 1) * tk, gi * dg:(gi + 1) * dg] = out.astype(o_ref.dtype)


def _fourier_mix(x, row_scale, mod, layer, gn, *, tk, nsub, tn):
    b, s, d = x.shape
    n = s // RADIX
    n_slab = tn // LANES
    mats = jnp.asarray(_seq_dft_mats(s, tk)).astype(BF16)
    cs = jnp.asarray(_chan_dft_mat()).astype(BF16)
    x_specs = [pl.BlockSpec((None, s, LANES), lambda bi, c, t, q=q: (bi, 0, c * n_slab + q)) for q in range(n_slab)]
    mod_cols = lambda which: pl.BlockSpec((None, None, None, 1, tn), lambda bi, c, t: (layer, bi, which, 0, c))
    y = pl.pallas_call(
        functools.partial(_fourier_kernel, n_slab=n_slab),
        grid=(b, d // tn, n // (tk * nsub)),
        in_specs=x_specs + [
            pl.BlockSpec((None, s, LANES), lambda bi, c, t: (bi, 0, 0)),
            pl.BlockSpec((1, tn), lambda bi, c, t: (0, c)),
            mod_cols(1),
            mod_cols(0),
            pl.BlockSpec((RADIX, nsub, 2 * tk, n), lambda bi, c, t: (0, t, 0, 0)),
            pl.BlockSpec((2 * FOURIER_GROUP_DIM, FOURIER_GROUP_DIM), lambda bi, c, t: (0, 0)),
        ],
        out_specs=pl.BlockSpec((None, RADIX, tk * nsub, tn), lambda bi, c, t: (bi, 0, t, c)),
        out_shape=jax.ShapeDtypeStruct((b, RADIX, n, d), BF16),
        compiler_params=_params("parallel", "parallel", "parallel"),
        name="fourier_mix",
    )(*([x] * n_slab), row_scale, gn, mod, mod, mats, cs)
    return y.reshape(b, s, d)


def _rope_tables(s):
    rows = s // GRID_W
    row = jnp.broadcast_to(jnp.arange(rows)[:, None], (rows, GRID_W)).reshape(-1)
    col = jnp.broadcast_to(jnp.arange(GRID_W)[None, :], (rows, GRID_W)).reshape(-1)
    n_freq = HEAD_DIM // 4
    inv_freq = ROPE_THETA ** (-jnp.arange(n_freq, dtype=F32) / n_freq)
    ang = jnp.concatenate([row.astype(F32)[:, None] * inv_freq, col.astype(F32)[:, None] * inv_freq], axis=-1)
    cos, sin = jnp.cos(ang), jnp.sin(ang)
    cosf = jnp.repeat(cos, 2, axis=-1)
    sinf = jnp.stack([-sin, sin], axis=-1).reshape(s, HEAD_DIM)
    return cosf, sinf


def _tiles(s, d, f):
    return dict(
        ts=min(1024, s), tq=min(512, s), tkv=min(256, s), tm_o=min(512, s), tm=min(512, s), tf=min(2048, f // 2),
        tk=min(ROW_SUB, s // RADIX), nsub=max(1, min(2, s // RADIX // ROW_SUB)), tn=min(512, d),
    )


def kernel(x, c, ctx, c_ctx, ada_w, ada_b, norm_mix_g, norm_mlp_g, attn_w_qkv, attn_q_norm_g, attn_k_norm_g,
           attn_w_o, fourier_w_o, mlp_w1, mlp_w2, final_norm_g):
    b, s, d = x.shape
    depth = ada_w.shape[0]
    assert depth == 2 and d % (KV_GROUP * HEAD_DIM) == 0 and s % GRID_W == 0 and b <= 4
    n_heads = d // HEAD_DIM
    n_kv = n_heads // KV_GROUP
    f = mlp_w1.shape[2]
    t = _tiles(s, d, f)

    c_rows = jnp.zeros((8, d), F32).at[:b].set(c).at[4].set(c_ctx)
    mod = _modulation(c_rows, ada_w, ada_b, tn=min(1024, N_MOD * d)).reshape(depth, 8, N_MOD, 1, d)

    w_qkv = attn_w_qkv[0].astype(BF16)
    gmix = norm_mix_g.reshape(depth, 1, d)
    gmlp = norm_mlp_g.reshape(depth, 1, d)
    qg = attn_q_norm_g[0].reshape(1, HEAD_DIM)
    kg = attn_k_norm_g[0].reshape(1, HEAD_DIM)
    cosf, sinf = _rope_tables(s)

    q, k_lat, v_lat = _qkv_project(x, mod, 0, None, gmix[0], w_qkv, qg, kg, cosf, sinf,
                                   n_q_heads=n_heads, n_kv_heads=n_kv, rope=True, ts=t["ts"])
    n_ctx = ctx.shape[1]
    k_ctx, v_ctx = _qkv_project(ctx, mod, 0, 4, gmix[0], w_qkv, qg, kg, cosf, sinf,
                                n_q_heads=0, n_kv_heads=n_kv, rope=False, ts=n_ctx)
    o, (w_ao, w_fo, w1, w2) = _attention(q, k_lat, v_lat, k_ctx, v_ctx,
                                         [attn_w_o[0], fourier_w_o[0], mlp_w1, mlp_w2], tq=t["tq"], tkv=t["tkv"])
    x1, h = _out_project(o, w_ao, x, mod, 0, gmlp[0], tm=t["tm_o"])
    x2, row_scale = _mlp(h, w1, w2, x1, mod, 0, None, tm=t["tm"], tf=t["tf"])

    y = _fourier_mix(x2, row_scale, mod, 1, gmix[1], tk=t["tk"], nsub=t["nsub"], tn=t["tn"])
    x1, h = _out_project(y, w_fo, x2, mod, 1, gmlp[1], tm=t["tm_o"])
    (out,) = _mlp(h, w1, w2, x1, mod, 1, final_norm_g.reshape(1, d), tm=t["tm"], tf=t["tf"])
    return out
```

```python
import functools

import numpy as np
import jax
import jax.numpy as jnp
from jax import lax
from jax.experimental import pallas as pl
from jax.experimental.pallas import tpu as pltpu

HEAD_DIM = 128
KV_GROUP = 4
GRID_W = 64
ROPE_THETA = 10000.0
FOURIER_GROUP_DIM = 256
N_MOD = 6
EPS = 1e-6
RADIX = 8
SQRT_HALF = 0.7071067811865476
LOG2_E = 1.4426950408889634
ROW_SUB = 256
LANES = 128
BF16_SUBLANES = 16
VMEM_LIMIT_BYTES = 60 * 1024 * 1024

F32 = jnp.float32
BF16 = jnp.bfloat16


def _params(*sem):
    return pltpu.CompilerParams(dimension_semantics=sem, vmem_limit_bytes=VMEM_LIMIT_BYTES)


def _resident(shape, index_map):
    return pl.BlockSpec(shape, index_map, pipeline_mode=pl.Buffered(1))


def _rms_scale(x):
    return lax.rsqrt(jnp.mean(x * x, axis=-1, keepdims=True) + EPS)


def _mod_kernel(c_ref, w_ref, b_ref, o_ref):
    c = c_ref[...]
    s = c / (1.0 + jnp.exp(-c))
    o_ref[...] = jnp.dot(s, w_ref[...], preferred_element_type=F32) + b_ref[...]


def _modulation(c_rows, ada_w, ada_b, tn):
    depth, d, n = ada_w.shape
    assert n % tn == 0
    return pl.pallas_call(
        _mod_kernel,
        grid=(depth, n // tn),
        in_specs=[
            pl.BlockSpec((8, d), lambda l, j: (0, 0)),
            pl.BlockSpec((None, d, tn), lambda l, j: (l, 0, j)),
            pl.BlockSpec((None, 1, tn), lambda l, j: (l, 0, j)),
        ],
        out_specs=pl.BlockSpec((None, 8, tn), lambda l, j: (l, 0, j)),
        out_shape=jax.ShapeDtypeStruct((depth, 8, n), F32),
        compiler_params=_params("parallel", "parallel"),
        name="adaln_modulation",
    )(c_rows, ada_w, ada_b.reshape(depth, 1, n))


def _mod_spec(d, layer, which, row=None):
    return pl.BlockSpec((None, None, None, 1, d),
                        lambda *g: (layer, g[0] if row is None else row, which, 0, 0))


def _qkv_kernel(x_ref, sh_ref, sc_ref, gn_ref, w_ref, qg_ref, kg_ref, cos_ref, sin_ref, *out_refs,
                n_q_heads, n_kv_heads, rope, q_scale, sub):
    if n_q_heads:
        q_ref, k_ref, v_ref = out_refs
    else:
        k_ref, v_ref = out_refs
    group_w = KV_GROUP * HEAD_DIM
    kw = n_kv_heads * HEAD_DIM
    gain_row = gn_ref[...] * (1.0 + sc_ref[...])
    for r in range(x_ref.shape[0] // sub):
        rows = slice(r * sub, (r + 1) * sub)
        x = x_ref[rows, :]
        h = (x * _rms_scale(x) * gain_row + sh_ref[...]).astype(BF16)
        if rope:
            cosf, sinf = cos_ref[rows, :], sin_ref[rows, :]
            even = (lax.broadcasted_iota(jnp.int32, cosf.shape, 1) % 2) == 0

        def head_norm_rope(t, gain):
            t = t * _rms_scale(t) * gain
            if rope:
                partner = jnp.where(even, pltpu.roll(t, HEAD_DIM - 1, 1), pltpu.roll(t, 1, 1))
                t = t * cosf + partner * sinf
            return t

        col = 0
        for g in range(n_q_heads // KV_GROUP):
            acc = jnp.dot(h, w_ref[:, col:col + group_w], preferred_element_type=F32)
            for i in range(KV_GROUP):
                t = head_norm_rope(acc[:, i * HEAD_DIM:(i + 1) * HEAD_DIM], qg_ref[...])
                q_ref[g * KV_GROUP + i, rows, :] = (t * q_scale).astype(BF16)
            col += group_w
        acc = jnp.dot(h, w_ref[:, col:col + kw], preferred_element_type=F32)
        for i in range(n_kv_heads):
            k_ref[i, rows, :] = head_norm_rope(acc[:, i * HEAD_DIM:(i + 1) * HEAD_DIM], kg_ref[...]).astype(BF16)
        col += kw
        acc = jnp.dot(h, w_ref[:, col:col + kw], preferred_element_type=F32)
        for i in range(n_kv_heads):
            v_ref[i, rows, :HEAD_DIM] = acc[:, i * HEAD_DIM:(i + 1) * HEAD_DIM].astype(BF16)
            v_ref[i, rows, HEAD_DIM:] = jnp.ones((sub, HEAD_DIM), BF16)


def _qkv_project(x, mod, layer, mod_row, gn, w, qg, kg, cosf, sinf, *, n_q_heads, n_kv_heads, rope, ts):
    b, s, d = x.shape
    ncols = (n_q_heads + 2 * n_kv_heads) * HEAD_DIM
    col_block = (w.shape[1] - ncols) // ncols
    assert col_block * ncols == w.shape[1] - ncols
    head_out = lambda nh, width=HEAD_DIM: pl.BlockSpec((None, nh, ts, width), lambda bi, i: (bi, 0, i, 0))
    head_shape = lambda nh, width=HEAD_DIM: jax.ShapeDtypeStruct((b, nh, s, width), BF16)
    out_specs = [head_out(n_kv_heads), head_out(n_kv_heads, 2 * HEAD_DIM)]
    out_shape = [head_shape(n_kv_heads), head_shape(n_kv_heads, 2 * HEAD_DIM)]
    if n_q_heads:
        out_specs.insert(0, head_out(n_q_heads))
        out_shape.insert(0, head_shape(n_q_heads))
    kernel = functools.partial(_qkv_kernel, n_q_heads=n_q_heads, n_kv_heads=n_kv_heads, rope=rope,
                               q_scale=HEAD_DIM ** -0.5 * LOG2_E, sub=min(ROW_SUB, ts))
    return pl.pallas_call(
        kernel,
        grid=(b, s // ts),
        in_specs=[
            pl.BlockSpec((None, ts, d), lambda bi, i: (bi, i, 0)),
            _mod_spec(d, layer, 0, mod_row),
            _mod_spec(d, layer, 1, mod_row),
            pl.BlockSpec((None, 1, d), lambda bi, i: (layer, 0, 0)),
            _resident((d, ncols), lambda bi, i: (0, col_block)),
            pl.BlockSpec((1, HEAD_DIM), lambda bi, i: (0, 0)),
            pl.BlockSpec((1, HEAD_DIM), lambda bi, i: (0, 0)),
            pl.BlockSpec((ts, HEAD_DIM), lambda bi, i: (i, 0)),
            pl.BlockSpec((ts, HEAD_DIM), lambda bi, i: (i, 0)),
        ],
        out_specs=out_specs,
        out_shape=out_shape,
        compiler_params=_params("parallel", "parallel"),
        name="qkv_project" if n_q_heads else "kv_project_ctx",
    )(x, mod, mod, gn, w, qg, kg, cosf, sinf)


def _attn_kernel(q_ref, kl_ref, vl_ref, kc_ref, vc_ref, *refs, tkv, n_cast):
    cast_in, o_ref, cast_out = refs[:n_cast], refs[n_cast], refs[n_cast + 1:]
    for src, dst in zip(cast_in, cast_out):
        dst[...] = src[...].astype(dst.dtype)
    g, tq, hd = q_ref.shape
    q = q_ref[...].reshape(g * tq, hd)
    chunks = [(kl_ref, vl_ref, i * tkv, tkv) for i in range(kl_ref.shape[0] // tkv)]
    chunks.append((kc_ref, vc_ref, 0, kc_ref.shape[0]))
    m = acc = None
    for k_ref, v_ref, off, n in chunks:
        s = lax.dot_general(q, k_ref[off:off + n, :], (((1,), (1,)), ((), ())), preferred_element_type=F32)
        m_chunk = jnp.max(s, axis=-1, keepdims=True)
        m_new = m_chunk if m is None else jnp.maximum(m, m_chunk)
        pv = jnp.dot(jnp.exp2(s - m_new).astype(BF16), v_ref[off:off + n, :], preferred_element_type=F32)
        acc = pv if m is None else jnp.exp2(m - m_new) * acc + pv
        m = m_new
    o = acc[:, :hd] / acc[:, hd:]
    for i in range(g):
        o_ref[:, i * hd:(i + 1) * hd] = o[i * tq:(i + 1) * tq].astype(o_ref.dtype)


def _attention(q, k_lat, v_lat, k_ctx, v_ctx, weights, *, tq, tkv):
    b, h, s, hd = q.shape
    kvh = h // KV_GROUP
    n_ctx = k_ctx.shape[2]
    n_i = s // tq
    steps = b * kvh * n_i
    kv_spec = lambda n, width=hd: pl.BlockSpec((None, None, n, width), lambda bi, j, i: (bi, j, 0, 0))
    sliced = [w.reshape(steps, -1, w.shape[-1]) for w in weights]
    for w in sliced:
        assert w.shape[1] % BF16_SUBLANES == 0
    cast_specs = [pl.BlockSpec((None,) + w.shape[1:], lambda bi, j, i: ((bi * kvh + j) * n_i + i, 0, 0))
                  for w in sliced]
    out = pl.pallas_call(
        functools.partial(_attn_kernel, tkv=tkv, n_cast=len(sliced)),
        grid=(b, kvh, n_i),
        in_specs=[
            pl.BlockSpec((None, KV_GROUP, tq, hd), lambda bi, j, i: (bi, j, i, 0)),
            kv_spec(s), kv_spec(s, 2 * hd), kv_spec(n_ctx), kv_spec(n_ctx, 2 * hd),
        ] + cast_specs,
        out_specs=[pl.BlockSpec((None, tq, KV_GROUP * hd), lambda bi, j, i: (bi, i, j))] + cast_specs,
        out_shape=[jax.ShapeDtypeStruct((b, s, h * hd), BF16)]
        + [jax.ShapeDtypeStruct(w.shape, BF16) for w in sliced],
        compiler_params=_params("parallel", "parallel", "parallel"),
        name="gqa_attention",
    )(q, k_lat, v_lat, k_ctx, v_ctx, *sliced)
    return out[0], [c.reshape(w.shape) for c, w in zip(out[1:], weights)]


def _oproj_kernel(a_ref, w_ref, x_ref, g_ref, gn_ref, sc_ref, sh_ref, x1_ref, h_ref, *, sub):
    gain_row = gn_ref[...] * (1.0 + sc_ref[...])
    for r in range(a_ref.shape[0] // sub):
        rows = slice(r * sub, (r + 1) * sub)
        y = jnp.dot(a_ref[rows, :], w_ref[...], preferred_element_type=F32)
        x1 = x_ref[rows, :] + g_ref[...] * y
        x1_ref[rows, :] = x1
        h_ref[rows, :] = (x1 * _rms_scale(x1) * gain_row + sh_ref[...]).astype(BF16)


def _out_project(a, w, x, mod, layer, gn, *, tm):
    b, s, d = x.shape
    tile = pl.BlockSpec((None, tm, d), lambda bi, i: (bi, i, 0))
    return pl.pallas_call(
        functools.partial(_oproj_kernel, sub=min(ROW_SUB, tm)),
        grid=(b, s // tm),
        in_specs=[
            tile,
            _resident((d, d), lambda bi, i: (0, 0)),
            tile,
            _mod_spec(d, layer, 2),
            pl.BlockSpec((None, 1, d), lambda bi, i: (layer, 0, 0)),
            _mod_spec(d, layer, 4),
            _mod_spec(d, layer, 3),
        ],
        out_specs=[tile, tile],
        out_shape=[jax.ShapeDtypeStruct((b, s, d), F32), jax.ShapeDtypeStruct((b, s, d), BF16)],
        compiler_params=_params("parallel", "parallel"),
        name="out_project",
    )(a, w, x, mod, gn, mod, mod)


def _mlp_kernel(h_ref, w1_ref, w2_ref, x1_ref, g2_ref, *refs, final, sub):
    c = pl.program_id(2)
    last = pl.num_programs(2) - 1
    if final:
        gn_ref, out_ref = refs
    else:
        out_ref, r_ref = refs
    row_blocks = [slice(r * sub, (r + 1) * sub) for r in range(h_ref.shape[0] // sub)]

    def chunk_sum(rows):
        u = jnp.maximum(jnp.dot(h_ref[rows, :], w1_ref[...], preferred_element_type=F32), 0.0)
        return jnp.dot((u * u).astype(BF16), w2_ref[...], preferred_element_type=F32)

    @pl.when(c == 0)
    def _():
        for rows in row_blocks:
            out_ref[rows, :] = chunk_sum(rows)

    @pl.when(jnp.logical_and(c > 0, c < last))
    def _():
        for rows in row_blocks:
            out_ref[rows, :] += chunk_sum(rows)

    @pl.when(c == last)
    def _():
        for rows in row_blocks:
            x2 = x1_ref[rows, :] + g2_ref[...] * (out_ref[rows, :] + chunk_sum(rows))
            scale = _rms_scale(x2)
            if final:
                out_ref[rows, :] = x2 * scale * gn_ref[...]
            else:
                out_ref[rows, :] = x2
                r_ref[rows, :] = jnp.broadcast_to(scale, (sub, LANES))


def _mlp(h, w1, w2, x1, mod, layer, gn_final, *, tm, tf):
    b, s, d = x1.shape
    f = w1.shape[2]
    assert f // tf >= 2
    final = gn_final is not None
    tile = pl.BlockSpec((None, tm, d), lambda bi, i, c: (bi, i, 0))
    out_shape = [jax.ShapeDtypeStruct((b, s, d), F32)]
    out_specs = [tile]
    extra_specs, extra_args = [], []
    if final:
        extra_specs.append(pl.BlockSpec((1, d), lambda bi, i, c: (0, 0)))
        extra_args.append(gn_final)
    else:
        out_shape.append(jax.ShapeDtypeStruct((b, s, LANES), F32))
        out_specs.append(pl.BlockSpec((None, tm, LANES), lambda bi, i, c: (bi, i, 0)))
    return pl.pallas_call(
        functools.partial(_mlp_kernel, final=final, sub=min(ROW_SUB, tm)),
        grid=(b, s // tm, f // tf),
        in_specs=[
            tile,
            pl.BlockSpec((None, d, tf), lambda bi, i, c: (layer, 0, c)),
            pl.BlockSpec((None, tf, d), lambda bi, i, c: (layer, c, 0)),
            tile,
            _mod_spec(d, layer, 5),
        ] + extra_specs,
        out_specs=out_specs,
        out_shape=out_shape,
        compiler_params=_params("parallel", "parallel", "arbitrary"),
        name="mlp_final" if final else "mlp",
    )(h, w1, w2, x1, mod, *extra_args)


def _seq_dft_mats(s, tk):
    n = s // RADIX
    k1 = np.arange(n, dtype=np.int64)[:, None]
    m = np.arange(n, dtype=np.int64)[None, :]
    mats = []
    for j in range(RADIX):
        ang = 2.0 * np.pi * (((RADIX * m + j) * k1) % s) / s
        c = (np.cos(ang) / np.sqrt(s)).reshape(n // tk, tk, n)
        sn = (-np.sin(ang) / np.sqrt(s)).reshape(n // tk, tk, n)
        mats.append(np.concatenate([c, sn], axis=1))
    return np.stack(mats).astype(np.float32)


def _chan_dft_mat():
    dg = FOURIER_GROUP_DIM
    idx = np.arange(dg, dtype=np.int64)
    ang = 2.0 * np.pi * ((idx[:, None] * idx[None, :]) % dg) / dg
    return (np.concatenate([np.cos(ang), np.sin(ang)], axis=0) / np.sqrt(dg)).astype(np.float32)


def _dft4(z):
    (ar, ai), (br, bi), (cr, ci), (dr, di) = z
    er, ei, fr, fi = ar + cr, ai + ci, ar - cr, ai - ci
    gr, gi, hr, hi = br + dr, bi + di, br - dr, bi - di
    return [(er + gr, ei + gi), (fr + hi, fi - hr), (er - gr, ei - gi), (fr - hi, fi + hr)]


def _fourier_kernel(*refs, n_slab):
    x_refs, (r_ref, gn_ref, sc_ref, sh_ref, m_ref, cs_ref, o_ref) = refs[:n_slab], refs[n_slab:]
    nsub, tk = m_ref.shape[1], m_ref.shape[2] // 2
    n = m_ref.shape[3]
    dg = FOURIER_GROUP_DIM
    slabs_per_group = dg // LANES
    cs = cs_ref[...]
    gain_row = gn_ref[...] * (1.0 + sc_ref[...])
    shift_row = sh_ref[...]
    scales = [r_ref[pl.ds(j, n, stride=RADIX), :] for j in range(RADIX)]
    def gather_planes(gi):
        planes = []
        for j in range(RADIX):
            cols = []
            for q in range(gi * slabs_per_group, (gi + 1) * slabs_per_group):
                lanes = slice(q * LANES, (q + 1) * LANES)
                xq = x_refs[q][pl.ds(j, n, stride=RADIX), :]
                cols.append((xq * scales[j] * gain_row[:, lanes] + shift_row[:, lanes]).astype(BF16))
            planes.append(jnp.concatenate(cols, axis=1))
        return planes

    def sequence_dft(planes, r):
        y = []
        for j in range(RADIX):
            yj = jnp.dot(m_ref[j, r], planes[j], preferred_element_type=F32)
            y.append((yj[:tk], yj[tk:]))
        return y

    def finish(y, gi, r):
        half = RADIX // 2
        u = [(y[j][0] + y[j + half][0], y[j][1] + y[j + half][1]) for j in range(half)]
        v = [(y[j][0] - y[j + half][0], y[j][1] - y[j + half][1]) for j in range(half)]
        w = [v[0],
             ((v[1][0] + v[1][1]) * SQRT_HALF, (v[1][1] - v[1][0]) * SQRT_HALF),
             (v[2][1], -v[2][0]),
             ((v[3][1] - v[3][0]) * SQRT_HALF, (-v[3][0] - v[3][1]) * SQRT_HALF)]
        even, odd = _dft4(u), _dft4(w)
        for k2 in range(RADIX):
            pr, pi = (even if k2 % 2 == 0 else odd)[k2 // 2]
            lhs = jnp.concatenate([pr.astype(BF16), pi.astype(BF16)], axis=1)
            out = jnp.dot(lhs, cs, preferred_element_type=F32)
            o_ref[k2, r * tk:(r + 1) * tk, gi * dg:(gi + 1) * dg] = out.astype(o_ref.dtype)

    chains = [(gi, r) for gi in range(n_slab // slabs_per_group) for r in range(nsub)]
    planes_of, pending = {}, None
    for gi, r in chains:
        if gi not in planes_of:
            planes_of[gi] = gather_planes(gi)
        y = sequence_dft(planes_of[gi], r)
        if pending is not None:
            finish(*pending)
        pending = (y, gi, r)
    finish(*pending)


def _fourier_mix(x, row_scale, mod, layer, gn, *, tk, nsub, tn):
    b, s, d = x.shape
    n = s // RADIX
    n_slab = tn // LANES
    mats = jnp.asarray(_seq_dft_mats(s, tk)).astype(BF16)
    cs = jnp.asarray(_chan_dft_mat()).astype(BF16)
    x_specs = [pl.BlockSpec((None, s, LANES), lambda bi, c, t, q=q: (bi, 0, c * n_slab + q)) for q in range(n_slab)]
    mod_cols = lambda which: pl.BlockSpec((None, None, None, 1, tn), lambda bi, c, t: (layer, bi, which, 0, c))
    y = pl.pallas_call(
        functools.partial(_fourier_kernel, n_slab=n_slab),
        grid=(b, d // tn, n // (tk * nsub)),
        in_specs=x_specs + [
            pl.BlockSpec((None, s, LANES), lambda bi, c, t: (bi, 0, 0)),
            pl.BlockSpec((None, 1, tn), lambda bi, c, t: (layer, 0, c)),
            mod_cols(1),
            mod_cols(0),
            pl.BlockSpec((RADIX, nsub, 2 * tk, n), lambda bi, c, t: (0, t, 0, 0)),
            pl.BlockSpec((2 * FOURIER_GROUP_DIM, FOURIER_GROUP_DIM), lambda bi, c, t: (0, 0)),
        ],
        out_specs=pl.BlockSpec((None, RADIX, tk * nsub, tn), lambda bi, c, t: (bi, 0, t, c)),
        out_shape=jax.ShapeDtypeStruct((b, RADIX, n, d), BF16),
        compiler_params=_params("parallel", "parallel", "parallel"),
        name="fourier_mix",
    )(*([x] * n_slab), row_scale, gn, mod, mod, mats, cs)
    return y.reshape(b, s, d)


def _rope_tables(s):
    rows = s // GRID_W
    row = jnp.broadcast_to(jnp.arange(rows)[:, None], (rows, GRID_W)).reshape(-1)
    col = jnp.broadcast_to(jnp.arange(GRID_W)[None, :], (rows, GRID_W)).reshape(-1)
    n_freq = HEAD_DIM // 4
    inv_freq = ROPE_THETA ** (-jnp.arange(n_freq, dtype=F32) / n_freq)
    ang = jnp.concatenate([row.astype(F32)[:, None] * inv_freq, col.astype(F32)[:, None] * inv_freq], axis=-1)
    cos, sin = jnp.cos(ang), jnp.sin(ang)
    cosf = jnp.repeat(cos, 2, axis=-1)
    sinf = jnp.stack([-sin, sin], axis=-1).reshape(s, HEAD_DIM)
    return cosf, sinf


def _tiles(s, d, f):
    return dict(
        ts=min(1024, s), tq=min(1024, s), tkv=min(256, s), tm_o=min(512, s), tm=min(512, s), tf=min(2048, f // 2),
        tk=min(ROW_SUB, s // RADIX), nsub=max(1, min(2, s // RADIX // ROW_SUB)), tn=min(512, d),
        tn_mod=1024,
    )


def kernel(x, c, ctx, c_ctx, ada_w, ada_b, norm_mix_g, norm_mlp_g, attn_w_qkv, attn_q_norm_g, attn_k_norm_g,
           attn_w_o, fourier_w_o, mlp_w1, mlp_w2, final_norm_g):
    b, s, d = x.shape
    depth = ada_w.shape[0]
    assert depth == 2 and d % (KV_GROUP * HEAD_DIM) == 0 and s % GRID_W == 0 and b <= 4
    n_heads = d // HEAD_DIM
    n_kv = n_heads // KV_GROUP
    f = mlp_w1.shape[2]
    t = _tiles(s, d, f)

    c_rows = jnp.concatenate([c, jnp.zeros((4 - b, d), F32), c_ctx[None, :], jnp.zeros((3, d), F32)], axis=0)
    mod = _modulation(c_rows, ada_w, ada_b, tn=t["tn_mod"]).reshape(depth, 8, N_MOD, 1, d)

    w_qkv = attn_w_qkv[0].astype(BF16)
    gmix = norm_mix_g.reshape(depth, 1, d)
    gmlp = norm_mlp_g.reshape(depth, 1, d)
    qg, kg = attn_q_norm_g, attn_k_norm_g
    cosf, sinf = _rope_tables(s)

    q, k_lat, v_lat = _qkv_project(x, mod, 0, None, gmix, w_qkv, qg, kg, cosf, sinf,
                                   n_q_heads=n_heads, n_kv_heads=n_kv, rope=True, ts=t["ts"])
    n_ctx = ctx.shape[1]
    k_ctx, v_ctx = _qkv_project(ctx, mod, 0, 4, gmix, w_qkv, qg, kg, cosf, sinf,
                                n_q_heads=0, n_kv_heads=n_kv, rope=False, ts=n_ctx)
    o, (w_ao, w_fo, w1, w2) = _attention(q, k_lat, v_lat, k_ctx, v_ctx,
                                         [attn_w_o[0], fourier_w_o[0], mlp_w1, mlp_w2], tq=t["tq"], tkv=t["tkv"])
    x1, h = _out_project(o, w_ao, x, mod, 0, gmlp, tm=t["tm_o"])
    x2, row_scale = _mlp(h, w1, w2, x1, mod, 0, None, tm=t["tm"], tf=t["tf"])

    y = _fourier_mix(x2, row_scale, mod, 1, gmix, tk=t["tk"], nsub=t["nsub"], tn=t["tn"])
    x1, h = _out_project(y, w_fo, x2, mod, 1, gmlp, tm=t["tm_o"])
    (out,) = _mlp(h, w1, w2, x1, mod, 1, final_norm_g.reshape(1, d), tm=t["tm"], tf=t["tf"])
    return out
```
